```python
import math
import jax, jax.numpy as jnp
from jax import lax
import numpy as np

D_MODEL = 1024
BATCH = 8
SEQ = 8192
DEPTH = 2
DEC_BATCH = 8
DEC_SEQ = 2048
PAST_LEN = 128

HEAD_DIM = 64
A_HEADS = 4
A_VDIM = 2 * HEAD_DIM
B_HEADS = 4
C_HEADS = 4
MEM_TOKENS = 256
GRID_W = 64
NA_ROWS = 8
NA_COLS = 16
T5_BUCKETS = 32
T5_MAX_DIST = 128
Q_BLOCK = 128
D_FF = 2816
CONV_WIDTH = 3
EPS = 1e-6

A_QK = A_HEADS * HEAD_DIM
A_V = A_HEADS * A_VDIM
B_W = B_HEADS * HEAD_DIM
C_W = C_HEADS * HEAD_DIM
MIX_WIDTH = A_V + B_W + C_W
IN_SPLITS = (A_QK, A_QK, A_QK, A_QK, A_V, B_W, B_W, B_W, C_W)
IN_WIDTH = 4 * A_QK + A_V + 3 * B_W + C_W

kernel_name = 'hymba_diff_natten_mem_encoder'


def rms_norm(x, g):
    xf = x.astype(jnp.float32)
    y = xf * lax.rsqrt(jnp.mean(xf * xf, axis=-1, keepdims=True) + EPS)
    return (y * g.astype(jnp.float32)).astype(x.dtype)


def t5_bucket(rp):
    half = T5_BUCKETS // 2
    max_exact = half // 2
    ret = jnp.where(rp > 0, half, 0)
    n = jnp.abs(rp)
    nf = jnp.maximum(n, 1).astype(jnp.float32)
    large = max_exact + (jnp.log(nf / max_exact) / math.log(T5_MAX_DIST / max_exact)
                         * (half - max_exact)).astype(jnp.int32)
    large = jnp.minimum(large, half - 1)
    return ret + jnp.where(n < max_exact, n, large)


def diff_attention(q1, q2, k1, k2, v, rel_bias, lam, lam_init, subln_g):
    B, S, H, _ = q1.shape
    nblk = S // Q_BLOCK
    scale = HEAD_DIM ** -0.5
    keys = jnp.arange(S)

    def to_blocks(t):
        return t.reshape(B, nblk, Q_BLOCK, H, t.shape[-1]).swapaxes(0, 1)

    def block(args):
        q1b, q2b, start = args
        rp = keys[None, :] - (start + jnp.arange(Q_BLOCK))[:, None]
        bias = jnp.transpose(rel_bias[t5_bucket(rp)], (2, 0, 1)).astype(jnp.float32)
        s1 = jnp.einsum('bqhd,bkhd->bhqk', q1b, k1).astype(jnp.float32) * scale + bias
        s2 = jnp.einsum('bqhd,bkhd->bhqk', q2b, k2).astype(jnp.float32) * scale + bias
        a = jax.nn.softmax(s1, axis=-1) - lam * jax.nn.softmax(s2, axis=-1)
        return jnp.einsum('bhqk,bkhe->bqhe', a.astype(v.dtype), v)

    o = lax.map(block, (to_blocks(q1), to_blocks(q2), jnp.arange(nblk) * Q_BLOCK))
    o = o.swapaxes(0, 1).reshape(B, S, H, A_VDIM)
    return rms_norm(o, subln_g) * (1.0 - lam_init)


def neighbourhood_attention(q, k, v, bias_tab):
    B, S, H, d = q.shape
    R = S // GRID_W
    KH = min(NA_ROWS, R)
    scale = d ** -0.5
    qg = q.reshape(B, R, GRID_W, H, d)
    kg = k.reshape(B, R, GRID_W, H, d)
    vg = v.reshape(B, R, GRID_W, H, d)
    cols = jnp.arange(GRID_W)
    cs = jnp.clip(cols - NA_COLS // 2, 0, GRID_W - NA_COLS)
    colidx = cs[:, None] + jnp.arange(NA_COLS)[None, :]
    dcol = colidx - cols[:, None] + (NA_COLS - 1)

    def row(r):
        rs = jnp.clip(r - KH // 2, 0, R - KH)
        kr = lax.dynamic_slice_in_dim(kg, rs, KH, axis=1)
        vr = lax.dynamic_slice_in_dim(vg, rs, KH, axis=1)
        kw = kr[:, :, colidx]
        vw = vr[:, :, colidx]
        qr = lax.dynamic_index_in_dim(qg, r, axis=1, keepdims=False)
        s = jnp.einsum('bchd,bicjhd->bhcij', qr, kw).astype(jnp.float32) * scale
        drow = rs + jnp.arange(KH) - r + (NA_ROWS - 1)
        bias = bias_tab[:, drow[:, None, None], dcol[None, :, :]]
        s = s + jnp.transpose(bias, (0, 2, 1, 3))[None].astype(jnp.float32)
        p = jax.nn.softmax(s.reshape(B, H, GRID_W, KH * NA_COLS), axis=-1)
        p = p.reshape(B, H, GRID_W, KH, NA_COLS).astype(v.dtype)
        return jnp.einsum('bhcij,bicjhe->bche', p, vw)

    o = lax.map(row, jnp.arange(R))
    return o.swapaxes(0, 1).reshape(B, S, H, d)


def memory_attention(q, mem, mem_g, w_mem_kv, kn):
    B, M, _ = mem.shape
    kv = rms_norm(mem, mem_g) @ w_mem_kv
    k, v = jnp.split(kv, 2, axis=-1)
    k = rms_norm(k.reshape(B, M, C_HEADS, HEAD_DIM), kn)
    v = v.reshape(B, M, C_HEADS, HEAD_DIM)
    s = jnp.einsum('bqhd,bmhd->bhqm', q, k).astype(jnp.float32) * (HEAD_DIM ** -0.5)
    p = jax.nn.softmax(s, axis=-1).astype(v.dtype)
    return jnp.einsum('bhqm,bmhd->bqhd', p, v)


def dwconv_centred(h, w, b):
    S = h.shape[1]
    hp = jnp.pad(h, ((0, 0), (CONV_WIDTH // 2, CONV_WIDTH // 2), (0, 0)))
    out = b
    for i in range(CONV_WIDTH):
        out = out + w[i] * hp[:, i:i + S]
    return out


def setup_inputs(seed: int = 0) -> dict:
    key = jax.random.key(seed)
    ks = jax.random.split(key, 32)
    nrm = jax.random.normal
    f32 = jnp.float32

    def gain(k, shape):
        return 1.0 + 0.02 * nrm(k, shape, f32)

    return {
        'x_prompt': nrm(ks[0], (BATCH, SEQ, D_MODEL), f32),
        'x_sample': nrm(ks[1], (DEC_BATCH, DEC_SEQ, D_MODEL), f32),
        'mem_prompt': nrm(ks[2], (BATCH, MEM_TOKENS, D_MODEL), f32),
        'mem_sample': nrm(ks[3], (DEC_BATCH, MEM_TOKENS, D_MODEL), f32),
        'norm1_g': gain(ks[4], (DEPTH, D_MODEL)),
        'w_in': nrm(ks[5], (DEPTH, D_MODEL, IN_WIDTH), f32) * D_MODEL ** -0.5,
        'qn_a': gain(ks[6], (DEPTH, HEAD_DIM)),
        'kn_a': gain(ks[7], (DEPTH, HEAD_DIM)),
        'lam_q1': 0.1 * nrm(ks[8], (DEPTH, HEAD_DIM), f32),
        'lam_k1': 0.1 * nrm(ks[9], (DEPTH, HEAD_DIM), f32),
        'lam_q2': 0.1 * nrm(ks[10], (DEPTH, HEAD_DIM), f32),
        'lam_k2': 0.1 * nrm(ks[11], (DEPTH, HEAD_DIM), f32),
        'subln_g': gain(ks[12], (DEPTH, A_VDIM)),
        'rel_bias': 0.5 * nrm(ks[13], (T5_BUCKETS, A_HEADS), f32),
        'qn_b': gain(ks[14], (DEPTH, HEAD_DIM)),
        'kn_b': gain(ks[15], (DEPTH, HEAD_DIM)),
        'na_bias': 0.5 * nrm(ks[16], (DEPTH, B_HEADS, 2 * NA_ROWS - 1, 2 * NA_COLS - 1), f32),
        'mem_g': gain(ks[17], (DEPTH, D_MODEL)),
        'w_mem_kv': nrm(ks[18], (DEPTH, D_MODEL, 2 * C_W), f32) * D_MODEL ** -0.5,
        'qn_c': gain(ks[19], (DEPTH, HEAD_DIM)),
        'kn_c': gain(ks[20], (DEPTH, HEAD_DIM)),
        'w_out': nrm(ks[21], (DEPTH, MIX_WIDTH, D_MODEL), f32) * MIX_WIDTH ** -0.5,
        'norm2_g': gain(ks[22], (DEPTH, D_MODEL)),
        'w_up': nrm(ks[23], (DEPTH, D_MODEL, 2 * D_FF), f32) * D_MODEL ** -0.5,
        'conv_w': nrm(ks[24], (DEPTH, CONV_WIDTH, 2 * D_FF), f32) * CONV_WIDTH ** -0.5,
        'conv_b': 0.01 * nrm(ks[25], (DEPTH, 2 * D_FF), f32),
        'w_down': nrm(ks[26], (DEPTH, D_FF, D_MODEL), f32) * D_FF ** -0.5,
    }


def reference(x_prompt, x_sample, mem_prompt, mem_sample, norm1_g, w_in, qn_a, kn_a,
              lam_q1, lam_k1, lam_q2, lam_k2, subln_g, rel_bias, qn_b, kn_b, na_bias,
              mem_g, w_mem_kv, qn_c, kn_c, w_out, norm2_g, w_up, conv_w, conv_b, w_down):
    split_at = [int(i) for i in np.cumsum(IN_SPLITS)[:-1]]

    def layer(x, mem, l):
        B, S, _ = x.shape
        h = rms_norm(x, norm1_g[l])
        z = h @ w_in[l]
        q1, q2, k1, k2, va, qb, kb, vb, qc = jnp.split(z, split_at, axis=-1)
        heads = lambda t, n: t.reshape(B, S, n, t.shape[-1] // n)
        lam_init = 0.8 - 0.6 * math.exp(-0.3 * l)
        lam = (jnp.exp(jnp.sum(lam_q1[l].astype(jnp.float32) * lam_k1[l].astype(jnp.float32)))
               - jnp.exp(jnp.sum(lam_q2[l].astype(jnp.float32) * lam_k2[l].astype(jnp.float32)))
               + lam_init)
        o_a = diff_attention(rms_norm(heads(q1, A_HEADS), qn_a[l]), rms_norm(heads(q2, A_HEADS), qn_a[l]),
                             rms_norm(heads(k1, A_HEADS), kn_a[l]), rms_norm(heads(k2, A_HEADS), kn_a[l]),
                             heads(va, A_HEADS), rel_bias, lam, lam_init, subln_g[l])
        o_b = neighbourhood_attention(rms_norm(heads(qb, B_HEADS), qn_b[l]),
                                      rms_norm(heads(kb, B_HEADS), kn_b[l]),
                                      heads(vb, B_HEADS), na_bias[l])
        o_c = memory_attention(rms_norm(heads(qc, C_HEADS), qn_c[l]), mem, mem_g[l], w_mem_kv[l], kn_c[l])
        mix = jnp.concatenate([o_a.reshape(B, S, A_V), o_b.reshape(B, S, B_W),
                               o_c.reshape(B, S, C_W)], axis=-1)
        x = x + mix @ w_out[l]
        u = dwconv_centred(rms_norm(x, norm2_g[l]) @ w_up[l], conv_w[l], conv_b[l])
        val, gate = jnp.split(u, 2, axis=-1)
        return x + (jax.nn.silu(gate) * val) @ w_down[l]

    y_prompt = x_prompt
    y_sample = x_sample
    for l in range(DEPTH):
        y_prompt = layer(y_prompt, mem_prompt, l)
    for l in range(DEPTH):
        y_sample = layer(y_sample, mem_sample, l)
    return (y_prompt, y_sample)
```

```python
import functools
import math

import numpy as np
import jax
import jax.numpy as jnp
from jax import lax
from jax.experimental import pallas as pl
from jax.experimental.pallas import tpu as pltpu

D_MODEL = 1024
HEAD_DIM = 64
A_HEADS = 4
A_VDIM = 2 * HEAD_DIM
B_HEADS = 4
C_HEADS = 4
MEM_TOKENS = 256
GRID_W = 64
NA_ROWS = 8
NA_COLS = 16
T5_BUCKETS = 32
T5_MAX_DIST = 128
D_FF = 2816
CONV_WIDTH = 3
EPS = 1e-6

A_QK = A_HEADS * HEAD_DIM
A_V = A_HEADS * A_VDIM
B_W = B_HEADS * HEAD_DIM
C_W = C_HEADS * HEAD_DIM
IN_WIDTH = 4 * A_QK + A_V + 3 * B_W + C_W

LANE_BLOCK = 256
NEG = -1e30
VMEM_LIMIT = 56 * 1024 * 1024

PROJ_ROWS = 512
ATT_T = 512
LOCAL_ROWS = 16
MEM_Q_TILE = 256
MIX_ROWS = 512
FFN_ROWS = 512
FF_CHUNK = 256
HALO = 8

f32 = jnp.float32
bf16 = jnp.bfloat16


def _rms(x, g):
    ms = jnp.mean(x * x, axis=-1, keepdims=True)
    return x * lax.rsqrt(ms + EPS) * g


def _group_normed(z, gmat, gain):
    msq = jnp.dot((z * z).astype(bf16), gmat, preferred_element_type=f32)
    return z * lax.rsqrt(msq + EPS) * gain


def _proj_kernel(x_ref, g1_ref, w_ref, gmat_ref, gain_ref,
                 qa_ref, ka_ref, va_ref, qb_ref, kb_ref, vb_ref, qc_ref):
    h = _rms(x_ref[0], g1_ref[...]).astype(bf16)
    gmat = gmat_ref[...]

    def block(n, normed):
        c0 = n * LANE_BLOCK
        z = jnp.dot(h, w_ref[:, c0:c0 + LANE_BLOCK], preferred_element_type=f32)
        if normed:
            z = _group_normed(z, gmat, gain_ref[:, c0:c0 + LANE_BLOCK])
        return z.astype(bf16)

    for pair in range(2):
        for ref, n, normed in ((qa_ref, pair, True), (ka_ref, 2 + pair, True), (va_ref, 4 + pair, False)):
            z = block(n, normed)
            ref[0, 2 * pair] = z[:, :128]
            ref[0, 2 * pair + 1] = z[:, 128:]
    qb_ref[0] = block(6, True)
    kb_ref[0] = block(7, True)
    vb_ref[0] = block(8, False)
    qc_ref[0] = block(9, True)


def _proj(x, g1, w, gmat, gain):
    B, S, _ = x.shape
    tm = PROJ_ROWS
    const = lambda b, i: (0, 0)
    head_spec = pl.BlockSpec((1, A_HEADS, tm, 128), lambda b, i: (b, 0, i, 0))
    tok_spec = pl.BlockSpec((1, tm, LANE_BLOCK), lambda b, i: (b, i, 0))
    head_shape = jax.ShapeDtypeStruct((B, A_HEADS, S, 128), bf16)
    tok_shape = jax.ShapeDtypeStruct((B, S, LANE_BLOCK), bf16)
    return pl.pallas_call(
        _proj_kernel,
        grid=(B, S // tm),
        in_specs=[
            pl.BlockSpec((1, tm, D_MODEL), lambda b, i: (b, i, 0)),
            pl.BlockSpec((1, D_MODEL), const),
            pl.BlockSpec((D_MODEL, IN_WIDTH), const),
            pl.BlockSpec((LANE_BLOCK, LANE_BLOCK), const),
            pl.BlockSpec((1, IN_WIDTH), const),
        ],
        out_specs=[head_spec, head_spec, head_spec, tok_spec, tok_spec, tok_spec, tok_spec],
        out_shape=[head_shape, head_shape, head_shape, tok_shape, tok_shape, tok_shape, tok_shape],
        compiler_params=pltpu.CompilerParams(
            dimension_semantics=("parallel", "parallel"), vmem_limit_bytes=VMEM_LIMIT),
        name="proj",
    )(x, g1, w, gmat, gain)


def _mem_kernel(m_ref, g_ref, w_ref, gmat_ref, gain_ref, kc_ref, vc_ref):
    h = _rms(m_ref[0], g_ref[...]).astype(bf16)
    z = jnp.dot(h, w_ref[...], preferred_element_type=f32)
    kc_ref[0] = _group_normed(z[:, :C_W], gmat_ref[...], gain_ref[...]).astype(bf16)
    vc_ref[0] = z[:, C_W:].astype(bf16)


def _mem_kv(mem, g, w, gmat, gain):
    B, M, _ = mem.shape
    const = lambda b: (0, 0)
    spec = pl.BlockSpec((1, M, C_W), lambda b: (b, 0, 0))
    shape = jax.ShapeDtypeStruct((B, M, C_W), bf16)
    return pl.pallas_call(
        _mem_kernel,
        grid=(B,),
        in_specs=[
            pl.BlockSpec((1, M, D_MODEL), lambda b: (b, 0, 0)),
            pl.BlockSpec((1, D_MODEL), const),
            pl.BlockSpec((D_MODEL, 2 * C_W), const),
            pl.BlockSpec((LANE_BLOCK, LANE_BLOCK), const),
            pl.BlockSpec((1, C_W), const),
        ],
        out_specs=[spec, spec],
        out_shape=[shape, shape],
        compiler_params=pltpu.CompilerParams(dimension_semantics=("parallel",)),
        name="mem_kv",
    )(mem, g, w, gmat, gain)


def _attn_a_kernel(cfar_ref, q_ref, k_ref, v_ref, bt_ref, lam_ref, sg_ref, o_ref,
                   vext_ref, qs_ref, acc_ref, m_ref, *, T, S, lam_init):
    nk = S // T
    h = pl.program_id(1)
    i = pl.program_id(2)

    @pl.when(i == 0)
    def _():
        vext_ref[:, :A_VDIM] = v_ref[0, 0]
        lane = lax.broadcasted_iota(jnp.int32, (S, 128), 1)
        vext_ref[:, A_VDIM:] = jnp.where(lane == 0, 1.0, 0.0).astype(bf16)

    q = q_ref[0, 0]
    lane = lax.broadcasted_iota(jnp.int32, (T, 128), 1)
    zero = jnp.zeros_like(q)
    qs_ref[:T] = jnp.where(lane < HEAD_DIM, q, zero)
    qs_ref[T:] = jnp.where(lane >= HEAD_DIM, q, zero)
    acc_ref[...] = jnp.zeros_like(acc_ref)
    m_ref[...] = jnp.full_like(m_ref, NEG)
    c_neg = cfar_ref[h, 0]
    c_pos = cfar_ref[h, 1]

    def step(j, cb, near):
        k0 = pl.multiple_of(j * T, T)
        s = lax.dot_general(qs_ref[...], k_ref[0, 0, pl.ds(k0, T), :],
                            (((1,), (1,)), ((), ())), preferred_element_type=f32)
        if near:
            bt = bt_ref[0, j - i + 1]
            s = (s.reshape(2, T, T) + bt[None]).reshape(2 * T, T)
        m_prev = m_ref[...]
        m_cur = jnp.max(s, axis=-1, keepdims=True)
        if not near:
            m_cur = m_cur + cb
        m_new = jnp.maximum(m_prev, m_cur)
        alpha = jnp.exp(m_prev - m_new)
        shift = m_new if near else m_new - cb
        p = jnp.exp(s - shift).astype(bf16)
        pv = jnp.dot(p, vext_ref[pl.ds(k0, T), :], preferred_element_type=f32)
        acc_ref[...] = alpha * acc_ref[...] + pv
        m_ref[...] = m_new

    def far_left(j, c):
        step(j, c_neg, False)
        return c

    def near(j, c):
        step(j, None, True)
        return c

    def far_right(j, c):
        step(j, c_pos, False)
        return c

    j_lo = jnp.maximum(i - 1, 0)
    j_hi = jnp.minimum(i + 2, nk)
    lax.fori_loop(0, j_lo, far_left, 0)
    lax.fori_loop(j_lo, j_hi, near, 0)
    lax.fori_loop(j_hi, nk, far_right, 0)

    acc = acc_ref[...]
    o1 = acc[:T, :A_VDIM]
    l1 = acc[:T, A_VDIM:A_VDIM + 1]
    o2 = acc[T:, :A_VDIM]
    l2 = acc[T:, A_VDIM:A_VDIM + 1]
    lam = (jnp.exp(jnp.sum(lam_ref[0:1] * lam_ref[1:2], axis=-1, keepdims=True))
           - jnp.exp(jnp.sum(lam_ref[2:3] * lam_ref[3:4], axis=-1, keepdims=True)) + lam_init)
    o = o1 / l1 - lam * (o2 / l2)
    o_ref[0] = (_rms(o, sg_ref[...]) * (1.0 - lam_init)).astype(bf16)


def _attn_a(qa, ka, va, btiles, cfar, lamv, sg, lam_init):
    B, H, S, _ = qa.shape
    T = ATT_T
    kernel = functools.partial(_attn_a_kernel, T=T, S=S, lam_init=lam_init)
    return pl.pallas_call(
        kernel,
        grid=(B, H, S // T),
        in_specs=[
            pl.BlockSpec(memory_space=pltpu.SMEM),
            pl.BlockSpec((1, 1, T, 128), lambda b, h, i: (b, h, i, 0)),
            pl.BlockSpec((1, 1, S, 128), lambda b, h, i: (b, h, 0, 0)),
            pl.BlockSpec((1, 1, S, 128), lambda b, h, i: (b, h, 0, 0)),
            pl.BlockSpec((1, 3, T, T), lambda b, h, i: (h, 0, 0, 0)),
            pl.BlockSpec((4, HEAD_DIM), lambda b, h, i: (0, 0)),
            pl.BlockSpec((1, A_VDIM), lambda b, h, i: (0, 0)),
        ],
        out_specs=pl.BlockSpec((1, T, A_VDIM), lambda b, h, i: (b, i, h)),
        out_shape=jax.ShapeDtypeStruct((B, S, A_V), bf16),
        scratch_shapes=[
            pltpu.VMEM((S, 2 * A_VDIM), bf16),
            pltpu.VMEM((2 * T, 128), bf16),
            pltpu.VMEM((2 * T, 2 * A_VDIM), f32),
            pltpu.VMEM((2 * T, 1), f32),
        ],
        compiler_params=pltpu.CompilerParams(
            dimension_semantics=("arbitrary", "arbitrary", "arbitrary"), vmem_limit_bytes=VMEM_LIMIT),
        name="attn_a",
    )(cfar, qa, ka, va, btiles, lamv, sg)


def _head_rows_attention(q, k, v, bias, nh):
    n = q.shape[0]
    rows = nh * n
    row_head = lax.broadcasted_iota(jnp.int32, (rows, LANE_BLOCK), 0) // n
    lane_head = lax.broadcasted_iota(jnp.int32, (rows, LANE_BLOCK), 1) // HEAD_DIM
    own = row_head == lane_head
    qs = jnp.where(own, jnp.concatenate([q] * nh, axis=0), jnp.zeros((rows, LANE_BLOCK), q.dtype))
    s = lax.dot_general(qs, k, (((1,), (1,)), ((), ())), preferred_element_type=f32)
    if bias is not None:
        s = s + bias
    m = jnp.max(s, axis=-1, keepdims=True)
    e = jnp.exp(s - m)
    l = jnp.sum(e, axis=-1, keepdims=True)
    r = jnp.dot(e.astype(bf16), v, preferred_element_type=f32) / l
    r = jnp.where(own, r, 0.0)
    out = r[:n]
    for hh in range(1, nh):
        out = out + r[hh * n:(hh + 1) * n]
    return out


def _local_kernel(qb_ref, kb_ref, vb_ref, bt_ref, qc_ref, kc_ref, vc_ref, ob_ref, oc_ref, *, R):
    g = pl.program_id(1)
    kh = NA_ROWS

    def row(rr, c):
        r = g * LOCAL_ROWS + rr
        rs = jnp.clip(r - kh // 2, 0, R - kh)
        q0 = pl.multiple_of(rr * GRID_W, GRID_W)
        k0 = pl.multiple_of(rs * GRID_W, GRID_W)
        q = qb_ref[0, pl.ds(q0, GRID_W), :]
        k = kb_ref[0, pl.ds(k0, kh * GRID_W), :]
        v = vb_ref[0, pl.ds(k0, kh * GRID_W), :]
        o = _head_rows_attention(q, k, v, bt_ref[r - rs], B_HEADS)
        ob_ref[0, pl.ds(q0, GRID_W), :] = o.astype(bf16)
        return c

    lax.fori_loop(0, LOCAL_ROWS, row, 0)

    kc = kc_ref[0]
    vc = vc_ref[0]

    def mem_tile(t, c):
        q0 = pl.multiple_of(t * MEM_Q_TILE, MEM_Q_TILE)
        o = _head_rows_attention(qc_ref[0, pl.ds(q0, MEM_Q_TILE), :], kc, vc, None, C_HEADS)
        oc_ref[0, pl.ds(q0, MEM_Q_TILE), :] = o.astype(bf16)
        return c

    lax.fori_loop(0, LOCAL_ROWS * GRID_W // MEM_Q_TILE, mem_tile, 0)


def _local(qb, kb, vb, na_tiles, qc, kc, vc):
    B, S, _ = qb.shape
    R = S // GRID_W
    rows = LOCAL_ROWS * GRID_W
    tile = pl.BlockSpec((1, rows, LANE_BLOCK), lambda b, g: (b, g, 0))
    full = pl.BlockSpec((1, S, LANE_BLOCK), lambda b, g: (b, 0, 0))
    memspec = pl.BlockSpec((1, MEM_TOKENS, LANE_BLOCK), lambda b, g: (b, 0, 0))
    shape = jax.ShapeDtypeStruct((B, S, LANE_BLOCK), bf16)
    return pl.pallas_call(
        functools.partial(_local_kernel, R=R),
        grid=(B, R // LOCAL_ROWS),
        in_specs=[tile, full, full,
                  pl.BlockSpec(na_tiles.shape, lambda b, g: (0, 0, 0)),
                  tile, memspec, memspec],
        out_specs=[tile, tile],
        out_shape=[shape, shape],
        compiler_params=pltpu.CompilerParams(
            dimension_semantics=("parallel", "parallel"), vmem_limit_bytes=VMEM_LIMIT),
        name="local",
    )(qb, kb, vb, na_tiles, qc, kc, vc)


def _mix_kernel(x_ref, oa_ref, ob_ref, oc_ref, wa_ref, wb_ref, wc_ref, o_ref):
    acc = jnp.dot(oa_ref[0], wa_ref[...], preferred_element_type=f32)
    acc = acc + jnp.dot(ob_ref[0], wb_ref[...], preferred_element_type=f32)
    acc = acc + jnp.dot(oc_ref[0], wc_ref[...], preferred_element_type=f32)
    o_ref[0] = x_ref[0] + acc


def _mix(x, oa, ob, oc, wa, wb, wc):
    B, S, _ = x.shape
    tm = MIX_ROWS
    const = lambda b, i: (0, 0)
    row = lambda w: pl.BlockSpec((1, tm, w), lambda b, i: (b, i, 0))
    return pl.pallas_call(
        _mix_kernel,
        grid=(B, S // tm),
        in_specs=[row(D_MODEL), row(A_V), row(B_W), row(C_W),
                  pl.BlockSpec((A_V, D_MODEL), const),
                  pl.BlockSpec((B_W, D_MODEL), const),
                  pl.BlockSpec((C_W, D_MODEL), const)],
        out_specs=row(D_MODEL),
        out_shape=jax.ShapeDtypeStruct((B, S, D_MODEL), f32),
        compiler_params=pltpu.CompilerParams(
            dimension_semantics=("parallel", "parallel"), vmem_limit_bytes=VMEM_LIMIT),
        name="mix",
    )(x, oa, ob, oc, wa, wb, wc)


def _ffn_kernel(x_ref, xp_ref, xn_ref, g2_ref, wv_ref, wg_ref, wd_ref, cp_ref, o_ref,
                h_ref, acc_ref, *, T, nT, nC):
    i = pl.program_id(1)
    g2 = g2_ref[...]
    x = x_ref[0]
    hp = jnp.where(i == 0, 0.0, _rms(xp_ref[0], g2))
    hn = jnp.where(i == nT - 1, 0.0, _rms(xn_ref[0], g2))
    h_ref[...] = jnp.concatenate([hp, _rms(x, g2), hn], axis=0).astype(bf16)
    acc_ref[...] = jnp.zeros_like(acc_ref)

    def conv(u, cp, r0, rb):
        return (cp[rb:rb + 1] + cp[r0:r0 + 1] * u[HALO - 1:HALO - 1 + T]
                + cp[r0 + 1:r0 + 2] * u[HALO:HALO + T] + cp[r0 + 2:r0 + 3] * u[HALO + 1:HALO + 1 + T])

    def chunk(c, carry):
        h = h_ref[...]
        cp = cp_ref[c]
        val = conv(jnp.dot(h, wv_ref[c], preferred_element_type=f32), cp, 0, 6)
        gate = conv(jnp.dot(h, wg_ref[c], preferred_element_type=f32), cp, 3, 7)
        act = (gate * jax.nn.sigmoid(gate) * val).astype(bf16)
        acc_ref[...] += jnp.dot(act, wd_ref[c], preferred_element_type=f32)
        return carry

    lax.fori_loop(0, nC, chunk, 0)
    o_ref[0] = x + acc_ref[...]


def _ffn(x, g2, wv, wg, wd, cp):
    B, S, _ = x.shape
    T = FFN_ROWS
    nT = S // T
    nC = wv.shape[0]
    hb = T // HALO
    const2 = lambda b, i: (0, 0)
    const3 = lambda b, i: (0, 0, 0)
    single = dict(pipeline_mode=pl.Buffered(1))
    return pl.pallas_call(
        functools.partial(_ffn_kernel, T=T, nT=nT, nC=nC),
        grid=(B, nT),
        in_specs=[
            pl.BlockSpec((1, T, D_MODEL), lambda b, i: (b, i, 0)),
            pl.BlockSpec((1, HALO, D_MODEL), lambda b, i: (b, jnp.maximum(i * hb - 1, 0), 0)),
            pl.BlockSpec((1, HALO, D_MODEL), lambda b, i: (b, jnp.minimum((i + 1) * hb, S // HALO - 1), 0)),
            pl.BlockSpec((1, D_MODEL), const2),
            pl.BlockSpec(wv.shape, const3, **single),
            pl.BlockSpec(wg.shape, const3, **single),
            pl.BlockSpec(wd.shape, const3, **single),
            pl.BlockSpec(cp.shape, const3),
        ],
        out_specs=pl.BlockSpec((1, T, D_MODEL), lambda b, i: (b, i, 0)),
        out_shape=jax.ShapeDtypeStruct((B, S, D_MODEL), f32),
        scratch_shapes=[pltpu.VMEM((T + 2 * HALO, D_MODEL), bf16), pltpu.VMEM((T, D_MODEL), f32)],
        compiler_params=pltpu.CompilerParams(
            dimension_semantics=("parallel", "parallel"), vmem_limit_bytes=VMEM_LIMIT),
        name="ffn",
    )(x, x, x, g2, wv, wg, wd, cp)


def _t5_bucket(rp):
    half = T5_BUCKETS // 2
    max_exact = half // 2
    ret = jnp.where(rp > 0, half, 0)
    n = jnp.abs(rp)
    nf = jnp.maximum(n, 1).astype(f32)
    large = max_exact + (jnp.log(nf / max_exact) / math.log(T5_MAX_DIST / max_exact)
                         * (half - max_exact)).astype(jnp.int32)
    large = jnp.minimum(large, half - 1)
    return ret + jnp.where(n < max_exact, n, large)


def _t5_tiles(rel_bias, T):
    assert T >= T5_MAX_DIST
    off = np.arange(T)[None, :] - np.arange(T)[:, None]
    rp = np.stack([off - T, off, off + T]).astype(np.int32)
    tiles = jnp.transpose(rel_bias[_t5_bucket(jnp.asarray(rp))], (3, 0, 1, 2)).astype(f32)
    far = rel_bias[_t5_bucket(jnp.asarray([-2 * T, 2 * T], jnp.int32))]
    return tiles, jnp.transpose(far).astype(f32)


def _na_tiles(na_bias_l):
    kh = NA_ROWS
    c = np.arange(GRID_W)
    cs = np.clip(c - NA_COLS // 2, 0, GRID_W - NA_COLS)
    kc = np.arange(GRID_W)
    valid = (kc[None, :] >= cs[:, None]) & (kc[None, :] < cs[:, None] + NA_COLS)
    dcol = np.clip(kc[None, :] - c[:, None] + (NA_COLS - 1), 0, 2 * NA_COLS - 2)
    delta = np.arange(kh)
    ki = np.arange(kh)
    drow = np.clip(ki[None, :] - delta[:, None] + (NA_ROWS - 1), 0, 2 * NA_ROWS - 2)
    vals = na_bias_l[:, drow[:, None, :, None], dcol[None, :, None, :]]
    vals = jnp.where(valid[None, None, :, None, :], vals.astype(f32), NEG)
    vals = jnp.transpose(vals, (1, 0, 2, 3, 4))
    return vals.reshape(kh, B_HEADS * GRID_W, kh * GRID_W)


def _in_perm():
    q1, q2, k1, k2 = 0, A_QK, 2 * A_QK, 3 * A_QK
    cols = []
    for a, b in ((q1, q2), (k1, k2)):
        for h in range(A_HEADS):
            cols += list(range(a + h * HEAD_DIM, a + (h + 1) * HEAD_DIM))
            cols += list(range(b + h * HEAD_DIM, b + (h + 1) * HEAD_DIM))
    cols += list(range(4 * A_QK, IN_WIDTH))
    return np.asarray(cols, np.int32)


def _layer_params(l, p):
    scale = HEAD_DIM ** -0.5
    ones = lambda n: jnp.ones((n,), f32)
    tile = lambda g, n: jnp.tile(g.astype(f32), n)
    gain = jnp.concatenate([
        tile(p['qn_a'][l], 2 * A_HEADS) * scale, tile(p['kn_a'][l], 2 * A_HEADS), ones(A_V),
        tile(p['qn_b'][l], B_HEADS) * scale, tile(p['kn_b'][l], B_HEADS), ones(B_W),
        tile(p['qn_c'][l], C_HEADS) * scale])[None, :]
    nC = D_FF // FF_CHUNK
    w_up = p['w_up'][l].astype(bf16)
    chunks = lambda w: jnp.transpose(w.reshape(D_MODEL, nC, FF_CHUNK), (1, 0, 2))
    cw = p['conv_w'][l].astype(f32)
    cb = p['conv_b'][l].astype(f32)
    cp = jnp.concatenate([cw[:, :D_FF], cw[:, D_FF:], cb[None, :D_FF], cb[None, D_FF:]], axis=0)
    w_out = p['w_out'][l].astype(bf16)
    return dict(
        g1=p['norm1_g'][l].astype(f32)[None, :],
        w_in=p['w_in'][l][:, _in_perm()].astype(bf16),
        gain=gain,
        lamv=jnp.stack([p['lam_q1'][l], p['lam_k1'][l], p['lam_q2'][l], p['lam_k2'][l]]).astype(f32),
        sg=p['subln_g'][l].astype(f32)[None, :],
        na=_na_tiles(p['na_bias'][l]),
        mem_g=p['mem_g'][l].astype(f32)[None, :],
        w_mem=p['w_mem_kv'][l].astype(bf16),
        gain_c=tile(p['kn_c'][l], C_HEADS)[None, :],
        wa=w_out[:A_V], wb=w_out[A_V:A_V + B_W], wc=w_out[A_V + B_W:],
        g2=p['norm2_g'][l].astype(f32)[None, :],
        wv=chunks(w_up[:, :D_FF]), wg=chunks(w_up[:, D_FF:]),
        wd=p['w_down'][l].astype(bf16).reshape(nC, FF_CHUNK, D_MODEL),
        cp=jnp.transpose(cp.reshape(8, nC, FF_CHUNK), (1, 0, 2)),
        lam_init=0.8 - 0.6 * math.exp(-0.3 * l),
    )


def _layer(x, mem, lp, shared):
    qa, ka, va, qb, kb, vb, qc = _proj(x, lp['g1'], lp['w_in'], shared['gmat'], lp['gain'])
    kc, vc = _mem_kv(mem, lp['mem_g'], lp['w_mem'], shared['gmat'], lp['gain_c'])
    oa = _attn_a(qa, ka, va, shared['btiles'], shared['cfar'], lp['lamv'], lp['sg'], lp['lam_init'])
    ob, oc = _local(qb, kb, vb, lp['na'], qc, kc, vc)
    x1 = _mix(x, oa, ob, oc, lp['wa'], lp['wb'], lp['wc'])
    return _ffn(x1, lp['g2'], lp['wv'], lp['wg'], lp['wd'], lp['cp'])


def kernel(x_prompt, x_sample, mem_prompt, mem_sample, norm1_g, w_in, qn_a, kn_a, lam_q1, lam_k1, lam_q2, lam_k2, subln_g, rel_bias, qn_b, kn_b, na_bias, mem_g, w_mem_kv, qn_c, kn_c, w_out, norm2_g, w_up, conv_w, conv_b, w_down):
    p = dict(norm1_g=norm1_g, w_in=w_in, qn_a=qn_a, kn_a=kn_a, lam_q1=lam_q1, lam_k1=lam_k1, lam_q2=lam_q2,
             lam_k2=lam_k2, subln_g=subln_g, qn_b=qn_b, kn_b=kn_b, na_bias=na_bias, mem_g=mem_g,
             w_mem_kv=w_mem_kv, qn_c=qn_c, kn_c=kn_c, w_out=w_out, norm2_g=norm2_g, w_up=w_up,
             conv_w=conv_w, conv_b=conv_b, w_down=w_down)
    depth = w_in.shape[0]
    group = np.arange(LANE_BLOCK) // HEAD_DIM
    gmat = jnp.asarray((group[:, None] == group[None, :]) / HEAD_DIM, bf16)
    btiles, cfar = _t5_tiles(rel_bias, ATT_T)
    shared = dict(gmat=gmat, btiles=btiles, cfar=cfar)
    layers = [_layer_params(l, p) for l in range(depth)]
    y_prompt, y_sample = x_prompt, x_sample
    for lp in layers:
        y_prompt = _layer(y_prompt, mem_prompt, lp, shared)
    for lp in layers:
        y_sample = _layer(y_sample, mem_sample, lp, shared)
    return (y_prompt, y_sample)
```

```python
import functools
import math

import numpy as np
import jax
import jax.numpy as jnp
from jax import lax
from jax.experimental import pallas as pl
from jax.experimental.pallas import tpu as pltpu

D_MODEL = 1024
HEAD_DIM = 64
A_HEADS = 4
A_VDIM = 2 * HEAD_DIM
B_HEADS = 4
C_HEADS = 4
MEM_TOKENS = 256
GRID_W = 64
NA_ROWS = 8
NA_COLS = 16
T5_BUCKETS = 32
T5_MAX_DIST = 128
D_FF = 2816
CONV_WIDTH = 3
EPS = 1e-6

A_QK = A_HEADS * HEAD_DIM
A_V = A_HEADS * A_VDIM
B_W = B_HEADS * HEAD_DIM
C_W = C_HEADS * HEAD_DIM
IN_WIDTH = 4 * A_QK + A_V + 3 * B_W + C_W

LANE_BLOCK = 256
NEG = -1e30
LOG2E = math.log2(math.e)
FAR_LANES = 32
VMEM_LIMIT = 56 * 1024 * 1024

PROJ_ROWS = 512
ATT_T = 512
PIPE_UNROLL = 4
LOCAL_ROWS = 16
MEM_Q_TILE = 256
MIX_ROWS = 512
FFN_ROWS = 512
FF_CHUNK = 256
HALO = 8

f32 = jnp.float32
bf16 = jnp.bfloat16


def _rms(x, g):
    ms = jnp.mean(x * x, axis=-1, keepdims=True)
    return x * lax.rsqrt(ms + EPS) * g


def _group_normed(z, gmat, gain):
    msq = jnp.dot((z * z).astype(bf16), gmat, preferred_element_type=f32)
    return z * lax.rsqrt(msq + EPS) * gain


def _proj_kernel(x_ref, g1_ref, w_ref, gmat_ref, gain_ref,
                 qa_ref, ka_ref, va_ref, qb_ref, kb_ref, vb_ref, qc_ref):
    h = _rms(x_ref[0], g1_ref[...]).astype(bf16)
    gmat = gmat_ref[...]

    def block(n, normed):
        c0 = n * LANE_BLOCK
        z = jnp.dot(h, w_ref[:, c0:c0 + LANE_BLOCK], preferred_element_type=f32)
        if normed:
            z = _group_normed(z, gmat, gain_ref[:, c0:c0 + LANE_BLOCK])
        return z.astype(bf16)

    for pair in range(2):
        for ref, n, normed in ((qa_ref, pair, True), (ka_ref, 2 + pair, True), (va_ref, 4 + pair, False)):
            z = block(n, normed)
            ref[0, 2 * pair] = z[:, :128]
            ref[0, 2 * pair + 1] = z[:, 128:]
    qb_ref[0] = block(6, True)
    kb_ref[0] = block(7, True)
    vb_ref[0] = block(8, False)
    qc_ref[0] = block(9, True)


def _proj(x, g1, w, gmat, gain):
    B, S, _ = x.shape
    tm = PROJ_ROWS
    const = lambda b, i: (0, 0)
    head_spec = pl.BlockSpec((1, A_HEADS, tm, 128), lambda b, i: (b, 0, i, 0))
    tok_spec = pl.BlockSpec((1, tm, LANE_BLOCK), lambda b, i: (b, i, 0))
    head_shape = jax.ShapeDtypeStruct((B, A_HEADS, S, 128), bf16)
    tok_shape = jax.ShapeDtypeStruct((B, S, LANE_BLOCK), bf16)
    return pl.pallas_call(
        _proj_kernel,
        grid=(B, S // tm),
        in_specs=[
            pl.BlockSpec((1, tm, D_MODEL), lambda b, i: (b, i, 0)),
            pl.BlockSpec((1, D_MODEL), const),
            pl.BlockSpec((D_MODEL, IN_WIDTH), const),
            pl.BlockSpec((LANE_BLOCK, LANE_BLOCK), const),
            pl.BlockSpec((1, IN_WIDTH), const),
        ],
        out_specs=[head_spec, head_spec, head_spec, tok_spec, tok_spec, tok_spec, tok_spec],
        out_shape=[head_shape, head_shape, head_shape, tok_shape, tok_shape, tok_shape, tok_shape],
        compiler_params=pltpu.CompilerParams(
            dimension_semantics=("parallel", "parallel"), vmem_limit_bytes=VMEM_LIMIT),
        name="proj",
    )(x, g1, w, gmat, gain)


def _mem_kernel(m_ref, g_ref, w_ref, gmat_ref, gain_ref, kc_ref, vc_ref):
    h = _rms(m_ref[0], g_ref[...]).astype(bf16)
    z = jnp.dot(h, w_ref[...], preferred_element_type=f32)
    kc_ref[0] = _group_normed(z[:, :C_W], gmat_ref[...], gain_ref[...]).astype(bf16)
    vc_ref[0] = z[:, C_W:].astype(bf16)


def _mem_kv(mem, g, w, gmat, gain):
    B, M, _ = mem.shape
    const = lambda b: (0, 0)
    spec = pl.BlockSpec((1, M, C_W), lambda b: (b, 0, 0))
    shape = jax.ShapeDtypeStruct((B, M, C_W), bf16)
    return pl.pallas_call(
        _mem_kernel,
        grid=(B,),
        in_specs=[
            pl.BlockSpec((1, M, D_MODEL), lambda b: (b, 0, 0)),
            pl.BlockSpec((1, D_MODEL), const),
            pl.BlockSpec((D_MODEL, 2 * C_W), const),
            pl.BlockSpec((LANE_BLOCK, LANE_BLOCK), const),
            pl.BlockSpec((1, C_W), const),
        ],
        out_specs=[spec, spec],
        out_shape=[shape, shape],
        compiler_params=pltpu.CompilerParams(dimension_semantics=("parallel",)),
        name="mem_kv",
    )(mem, g, w, gmat, gain)


def _attn_a_kernel(cfar_ref, q_ref, k_ref, v_ref, bt_ref, lam_ref, sg_ref, o_ref,
                   kext_ref, vext_ref, qs_ref, sa_ref, sb_ref, acc_ref, m_ref, *, T, S, lam_init):
    nk = S // T
    nc = T // 128
    h = pl.program_id(1)
    i = pl.program_id(2)

    @pl.when(i == 0)
    def _():
        row = lax.broadcasted_iota(jnp.int32, (S, 128), 0)
        lane = lax.broadcasted_iota(jnp.int32, (S, 128), 1)
        kext_ref[:, :128] = k_ref[0, 0]
        onehot = (lane < 2 * FAR_LANES) & ((lane & (FAR_LANES - 1)) == row // T)
        kext_ref[:, 128:] = jnp.where(onehot, 1.0, 0.0).astype(bf16)
        vext_ref[:, :A_VDIM] = v_ref[0, 0]
        vext_ref[:, A_VDIM:] = jnp.where(lane == 0, 1.0, 0.0).astype(bf16)

    q = q_ref[0, 0]
    lane = lax.broadcasted_iota(jnp.int32, (T, 128), 1)
    zero = jnp.zeros_like(q)
    qs_ref[:T, :128] = jnp.where(lane < HEAD_DIM, q, zero)
    qs_ref[T:, :128] = jnp.where(lane >= HEAD_DIM, q, zero)
    chunk = lane & (FAR_LANES - 1)
    c = jnp.where(chunk <= i - 2, cfar_ref[h, 0], jnp.where(chunk >= i + 2, cfar_ref[h, 1], 0.0))
    c_hi = c.astype(bf16)
    c_lo = (c - c_hi.astype(f32)).astype(bf16)
    ext = jnp.where(lane < FAR_LANES, c_hi, jnp.where(lane < 2 * FAR_LANES, c_lo, zero))
    qs_ref[:T, 128:] = ext
    qs_ref[T:, 128:] = ext
    acc_ref[...] = jnp.zeros_like(acc_ref)
    m_ref[...] = jnp.full_like(m_ref, NEG)

    def scores(j):
        k0 = pl.multiple_of(j * T, T)
        return lax.dot_general(qs_ref[...], kext_ref[pl.ds(k0, T), :],
                               (((1,), (1,)), ((), ())), preferred_element_type=f32)

    def softmax_pv(j, s_ref, with_tile):
        k0 = pl.multiple_of(j * T, T)
        cols = [s_ref[:, n * 128:(n + 1) * 128] for n in range(nc)]
        if with_tile:
            near = jnp.abs(j - i) <= 1
            tile = jnp.where(near, j - i + 1, 3)
            cols = [(col.reshape(2, T, 128) + bt_ref[0, tile, :, n * 128:(n + 1) * 128][None]).reshape(2 * T, 128)
                    for n, col in enumerate(cols)]
        mx = cols[0]
        for n in range(1, nc):
            mx = jnp.maximum(mx, cols[n])
        m_prev = m_ref[...]
        m_new = jnp.maximum(m_prev, jnp.broadcast_to(jnp.max(mx, axis=-1, keepdims=True), m_prev.shape))
        alpha = jnp.exp2(m_prev - m_new)
        p = jnp.concatenate([jnp.exp2(col - m_new).astype(bf16) for col in cols], axis=1)
        pv = jnp.dot(p, vext_ref[pl.ds(k0, T), :], preferred_element_type=f32)
        acc_ref[:, :128] = alpha * acc_ref[:, :128] + pv[:, :128]
        acc_ref[:, 128:] = alpha * acc_ref[:, 128:] + pv[:, 128:]
        m_ref[...] = m_new

    lo = jnp.maximum(i - 1, 0)
    n_near = jnp.minimum(i + 2, nk) - lo
    n_far = nk - n_near

    def chunk_at(t):
        return jnp.where(t < n_far, jnp.where(t < lo, t, t + n_near), lo + t - n_far)

    bufs = (sa_ref, sb_ref)
    sa_ref[...] = scores(chunk_at(0))

    def block(t0, n, with_tile):
        for u in range(n):
            if not (with_tile and u == n - 1):
                bufs[(u + 1) % 2][...] = scores(chunk_at(t0 + u + 1))
            softmax_pv(chunk_at(t0 + u), bufs[u % 2], with_tile)

    def main(g, carry):
        block(g * PIPE_UNROLL, PIPE_UNROLL, False)
        return carry

    lax.fori_loop(0, (nk - PIPE_UNROLL) // PIPE_UNROLL, main, 0)
    block(nk - PIPE_UNROLL, PIPE_UNROLL, True)

    acc = acc_ref[...]
    o1 = acc[:T, :A_VDIM]
    l1 = acc[:T, A_VDIM:A_VDIM + 1]
    o2 = acc[T:, :A_VDIM]
    l2 = acc[T:, A_VDIM:A_VDIM + 1]
    lam = (jnp.exp(jnp.sum(lam_ref[0:1] * lam_ref[1:2], axis=-1, keepdims=True))
           - jnp.exp(jnp.sum(lam_ref[2:3] * lam_ref[3:4], axis=-1, keepdims=True)) + lam_init)
    o = o1 / l1 - lam * (o2 / l2)
    o_ref[0] = (_rms(o, sg_ref[...]) * (1.0 - lam_init)).astype(bf16)


def _attn_a(qa, ka, va, btiles, cfar, lamv, sg, lam_init):
    B, H, S, _ = qa.shape
    T = ATT_T
    assert S % (PIPE_UNROLL * T) == 0 and S // T <= FAR_LANES
    kernel = functools.partial(_attn_a_kernel, T=T, S=S, lam_init=lam_init)
    return pl.pallas_call(
        kernel,
        grid=(B, H, S // T),
        in_specs=[
            pl.BlockSpec(memory_space=pltpu.SMEM),
            pl.BlockSpec((1, 1, T, 128), lambda b, h, i: (b, h, i, 0)),
            pl.BlockSpec((1, 1, S, 128), lambda b, h, i: (b, h, 0, 0)),
            pl.BlockSpec((1, 1, S, 128), lambda b, h, i: (b, h, 0, 0)),
            pl.BlockSpec((1, 4, T, T), lambda b, h, i: (h, 0, 0, 0)),
            pl.BlockSpec((4, HEAD_DIM), lambda b, h, i: (0, 0)),
            pl.BlockSpec((1, A_VDIM), lambda b, h, i: (0, 0)),
        ],
        out_specs=pl.BlockSpec((1, T, A_VDIM), lambda b, h, i: (b, i, h)),
        out_shape=jax.ShapeDtypeStruct((B, S, A_V), bf16),
        scratch_shapes=[
            pltpu.VMEM((S, 256), bf16),
            pltpu.VMEM((S, 2 * A_VDIM), bf16),
            pltpu.VMEM((2 * T, 256), bf16),
            pltpu.VMEM((2 * T, T), f32),
            pltpu.VMEM((2 * T, T), f32),
            pltpu.VMEM((2 * T, 2 * A_VDIM), f32),
            pltpu.VMEM((2 * T, 128), f32),
        ],
        compiler_params=pltpu.CompilerParams(
            dimension_semantics=("arbitrary", "arbitrary", "arbitrary"), vmem_limit_bytes=VMEM_LIMIT),
        name="attn_a",
    )(cfar, qa, ka, va, btiles, lamv, sg)


def _head_rows_attention(q, k, v, bias, nh):
    n = q.shape[0]
    rows = nh * n
    row_head = lax.broadcasted_iota(jnp.int32, (rows, LANE_BLOCK), 0) // n
    lane_head = lax.broadcasted_iota(jnp.int32, (rows, LANE_BLOCK), 1) // HEAD_DIM
    own = row_head == lane_head
    qs = jnp.where(own, jnp.concatenate([q] * nh, axis=0), jnp.zeros((rows, LANE_BLOCK), q.dtype))
    s = lax.dot_general(qs, k, (((1,), (1,)), ((), ())), preferred_element_type=f32)
    if bias is not None:
        s = s + bias
    m = jnp.max(s, axis=-1, keepdims=True)
    e = jnp.exp(s - m)
    l = jnp.sum(e, axis=-1, keepdims=True)
    r = jnp.dot(e.astype(bf16), v, preferred_element_type=f32) / l
    r = jnp.where(own, r, 0.0)
    out = r[:n]
    for hh in range(1, nh):
        out = out + r[hh * n:(hh + 1) * n]
    return out


def _local_kernel(qb_ref, kb_ref, vb_ref, bt_ref, qc_ref, kc_ref, vc_ref, ob_ref, oc_ref, *, R):
    g = pl.program_id(1)
    kh = NA_ROWS

    def row(rr, c):
        r = g * LOCAL_ROWS + rr
        rs = jnp.clip(r - kh // 2, 0, R - kh)
        q0 = pl.multiple_of(rr * GRID_W, GRID_W)
        k0 = pl.multiple_of(rs * GRID_W, GRID_W)
        q = qb_ref[0, pl.ds(q0, GRID_W), :]
        k = kb_ref[0, pl.ds(k0, kh * GRID_W), :]
        v = vb_ref[0, pl.ds(k0, kh * GRID_W), :]
        o = _head_rows_attention(q, k, v, bt_ref[r - rs], B_HEADS)
        ob_ref[0, pl.ds(q0, GRID_W), :] = o.astype(bf16)
        return c

    lax.fori_loop(0, LOCAL_ROWS, row, 0)

    kc = kc_ref[0]
    vc = vc_ref[0]

    def mem_tile(t, c):
        q0 = pl.multiple_of(t * MEM_Q_TILE, MEM_Q_TILE)
        o = _head_rows_attention(qc_ref[0, pl.ds(q0, MEM_Q_TILE), :], kc, vc, None, C_HEADS)
        oc_ref[0, pl.ds(q0, MEM_Q_TILE), :] = o.astype(bf16)
        return c

    lax.fori_loop(0, LOCAL_ROWS * GRID_W // MEM_Q_TILE, mem_tile, 0)


def _local(qb, kb, vb, na_tiles, qc, kc, vc):
    B, S, _ = qb.shape
    R = S // GRID_W
    rows = LOCAL_ROWS * GRID_W
    tile = pl.BlockSpec((1, rows, LANE_BLOCK), lambda b, g: (b, g, 0))
    full = pl.BlockSpec((1, S, LANE_BLOCK), lambda b, g: (b, 0, 0))
    memspec = pl.BlockSpec((1, MEM_TOKENS, LANE_BLOCK), lambda b, g: (b, 0, 0))
    shape = jax.ShapeDtypeStruct((B, S, LANE_BLOCK), bf16)
    return pl.pallas_call(
        functools.partial(_local_kernel, R=R),
        grid=(B, R // LOCAL_ROWS),
        in_specs=[tile, full, full,
                  pl.BlockSpec(na_tiles.shape, lambda b, g: (0, 0, 0)),
                  tile, memspec, memspec],
        out_specs=[tile, tile],
        out_shape=[shape, shape],
        compiler_params=pltpu.CompilerParams(
            dimension_semantics=("parallel", "parallel"), vmem_limit_bytes=VMEM_LIMIT),
        name="local",
    )(qb, kb, vb, na_tiles, qc, kc, vc)


def _mix_kernel(x_ref, oa_ref, ob_ref, oc_ref, wa_ref, wb_ref, wc_ref, o_ref):
    acc = jnp.dot(oa_ref[0], wa_ref[...], preferred_element_type=f32)
    acc = acc + jnp.dot(ob_ref[0], wb_ref[...], preferred_element_type=f32)
    acc = acc + jnp.dot(oc_ref[0], wc_ref[...], preferred_element_type=f32)
    o_ref[0] = x_ref[0] + acc


def _mix(x, oa, ob, oc, wa, wb, wc):
    B, S, _ = x.shape
    tm = MIX_ROWS
    const = lambda b, i: (0, 0)
    row = lambda w: pl.BlockSpec((1, tm, w), lambda b, i: (b, i, 0))
    return pl.pallas_call(
        _mix_kernel,
        grid=(B, S // tm),
        in_specs=[row(D_MODEL), row(A_V), row(B_W), row(C_W),
                  pl.BlockSpec((A_V, D_MODEL), const),
                  pl.BlockSpec((B_W, D_MODEL), const),
                  pl.BlockSpec((C_W, D_MODEL), const)],
        out_specs=row(D_MODEL),
        out_shape=jax.ShapeDtypeStruct((B, S, D_MODEL), f32),
        compiler_params=pltpu.CompilerParams(
            dimension_semantics=("parallel", "parallel"), vmem_limit_bytes=VMEM_LIMIT),
        name="mix",
    )(x, oa, ob, oc, wa, wb, wc)


def _ffn_kernel(x_ref, xp_ref, xn_ref, g2_ref, wv_ref, wg_ref, wd_ref, cp_ref, o_ref,
                h_ref, acc_ref, *, T, nT, nC):
    i = pl.program_id(1)
    g2 = g2_ref[...]
    x = x_ref[0]
    hp = jnp.where(i == 0, 0.0, _rms(xp_ref[0], g2))
    hn = jnp.where(i == nT - 1, 0.0, _rms(xn_ref[0], g2))
    h_ref[...] = jnp.concatenate([hp, _rms(x, g2), hn], axis=0).astype(bf16)
    acc_ref[...] = jnp.zeros_like(acc_ref)

    def conv(u, cp, r0, rb):
        return (cp[rb:rb + 1] + cp[r0:r0 + 1] * u[HALO - 1:HALO - 1 + T]
                + cp[r0 + 1:r0 + 2] * u[HALO:HALO + T] + cp[r0 + 2:r0 + 3] * u[HALO + 1:HALO + 1 + T])

    def chunk(c, carry):
        h = h_ref[...]
        cp = cp_ref[c]
        val = conv(jnp.dot(h, wv_ref[c], preferred_element_type=f32), cp, 0, 6)
        gate = conv(jnp.dot(h, wg_ref[c], preferred_element_type=f32), cp, 3, 7)
        act = (gate * jax.nn.sigmoid(gate) * val).astype(bf16)
        acc_ref[...] += jnp.dot(act, wd_ref[c], preferred_element_type=f32)
        return carry

    lax.fori_loop(0, nC, chunk, 0)
    o_ref[0] = x + acc_ref[...]


def _ffn(x, g2, wv, wg, wd, cp):
    B, S, _ = x.shape
    T = FFN_ROWS
    nT = S // T
    nC = wv.shape[0]
    hb = T // HALO
    const2 = lambda b, i: (0, 0)
    const3 = lambda b, i: (0, 0, 0)
    single = dict(pipeline_mode=pl.Buffered(1))
    return pl.pallas_call(
        functools.partial(_ffn_kernel, T=T, nT=nT, nC=nC),
        grid=(B, nT),
        in_specs=[
            pl.BlockSpec((1, T, D_MODEL), lambda b, i: (b, i, 0)),
            pl.BlockSpec((1, HALO, D_MODEL), lambda b, i: (b, jnp.maximum(i * hb - 1, 0), 0)),
            pl.BlockSpec((1, HALO, D_MODEL), lambda b, i: (b, jnp.minimum((i + 1) * hb, S // HALO - 1), 0)),
            pl.BlockSpec((1, D_MODEL), const2),
            pl.BlockSpec(wv.shape, const3, **single),
            pl.BlockSpec(wg.shape, const3, **single),
            pl.BlockSpec(wd.shape, const3, **single),
            pl.BlockSpec(cp.shape, const3),
        ],
        out_specs=pl.BlockSpec((1, T, D_MODEL), lambda b, i: (b, i, 0)),
        out_shape=jax.ShapeDtypeStruct((B, S, D_MODEL), f32),
        scratch_shapes=[pltpu.VMEM((T + 2 * HALO, D_MODEL), bf16), pltpu.VMEM((T, D_MODEL), f32)],
        compiler_params=pltpu.CompilerParams(
            dimension_semantics=("parallel", "parallel"), vmem_limit_bytes=VMEM_LIMIT),
        name="ffn",
    )(x, x, x, g2, wv, wg, wd, cp)


def _t5_bucket(rp):
    half = T5_BUCKETS // 2
    max_exact = half // 2
    ret = jnp.where(rp > 0, half, 0)
    n = jnp.abs(rp)
    nf = jnp.maximum(n, 1).astype(f32)
    large = max_exact + (jnp.log(nf / max_exact) / math.log(T5_MAX_DIST / max_exact)
                         * (half - max_exact)).astype(jnp.int32)
    large = jnp.minimum(large, half - 1)
    return ret + jnp.where(n < max_exact, n, large)


def _t5_tiles(rel_bias, T):
    assert T >= T5_MAX_DIST
    off = np.arange(T)[None, :] - np.arange(T)[:, None]
    rp = np.stack([off - T, off, off + T]).astype(np.int32)
    bucket = _t5_bucket(jnp.asarray(rp))[None]
    table = jnp.transpose(rel_bias).astype(f32) * LOG2E
    tiles = jnp.zeros((table.shape[0], 3, T, T), f32)
    for b in range(T5_BUCKETS):
        tiles = jnp.where(bucket == b, table[:, b][:, None, None, None], tiles)
    tiles = jnp.concatenate([tiles, jnp.zeros_like(tiles[:, :1])], axis=1)
    far_bucket = _t5_bucket(jnp.asarray([-2 * T, 2 * T], jnp.int32))
    far = jnp.zeros((table.shape[0], 2), f32)
    for b in range(T5_BUCKETS):
        far = jnp.where(far_bucket[None, :] == b, table[:, b][:, None], far)
    return tiles, far


def _na_tiles(na_bias_l):
    kh = NA_ROWS
    c = np.arange(GRID_W)
    cs = np.clip(c - NA_COLS // 2, 0, GRID_W - NA_COLS)
    kc = np.arange(GRID_W)
    valid = (kc[None, :] >= cs[:, None]) & (kc[None, :] < cs[:, None] + NA_COLS)
    dcol = kc[None, :] - c[:, None] + (NA_COLS - 1)
    delta = np.arange(kh)
    ki = np.arange(kh)
    drow = ki[None, :] - delta[:, None] + (NA_ROWS - 1)
    row_sel = (drow[:, :, None] == np.arange(2 * NA_ROWS - 1)).astype(np.float32)
    col_sel = (dcol[None] == np.arange(2 * NA_COLS - 1)[:, None, None]).astype(np.float32)
    hp = lax.Precision.HIGHEST
    tmp = jnp.einsum('hab,bck->hack', na_bias_l.astype(f32), col_sel, precision=hp)
    vals = jnp.einsum('dia,hack->dhcik', row_sel, tmp, precision=hp)
    vals = jnp.where(valid[None, None, :, None, :], vals, NEG)
    return vals.reshape(kh, B_HEADS * GRID_W, kh * GRID_W)


def _in_perm():
    q1, q2, k1, k2 = 0, A_QK, 2 * A_QK, 3 * A_QK
    cols = []
    for a, b in ((q1, q2), (k1, k2)):
        for h in range(A_HEADS):
            cols += list(range(a + h * HEAD_DIM, a + (h + 1) * HEAD_DIM))
            cols += list(range(b + h * HEAD_DIM, b + (h + 1) * HEAD_DIM))
    cols += list(range(4 * A_QK, IN_WIDTH))
    return np.asarray(cols, np.int32)


def _layer_params(l, p):
    scale = HEAD_DIM ** -0.5
    ones = lambda n: jnp.ones((n,), f32)
    tile = lambda g, n: jnp.tile(g.astype(f32), n)
    gain = jnp.concatenate([
        tile(p['qn_a'][l], 2 * A_HEADS) * (scale * LOG2E), tile(p['kn_a'][l], 2 * A_HEADS), ones(A_V),
        tile(p['qn_b'][l], B_HEADS) * scale, tile(p['kn_b'][l], B_HEADS), ones(B_W),
        tile(p['qn_c'][l], C_HEADS) * scale])[None, :]
    nC = D_FF // FF_CHUNK
    w_up = p['w_up'][l].astype(bf16)
    chunks = lambda w: jnp.transpose(w.reshape(D_MODEL, nC, FF_CHUNK), (1, 0, 2))
    cw = p['conv_w'][l].astype(f32)
    cb = p['conv_b'][l].astype(f32)
    cp = jnp.concatenate([cw[:, :D_FF], cw[:, D_FF:], cb[None, :D_FF], cb[None, D_FF:]], axis=0)
    w_out = p['w_out'][l].astype(bf16)
    return dict(
        g1=p['norm1_g'][l].astype(f32)[None, :],
        w_in=p['w_in'][l][:, _in_perm()].astype(bf16),
        gain=gain,
        lamv=jnp.stack([p['lam_q1'][l], p['lam_k1'][l], p['lam_q2'][l], p['lam_k2'][l]]).astype(f32),
        sg=p['subln_g'][l].astype(f32)[None, :],
        na=_na_tiles(p['na_bias'][l]),
        mem_g=p['mem_g'][l].astype(f32)[None, :],
        w_mem=p['w_mem_kv'][l].astype(bf16),
        gain_c=tile(p['kn_c'][l], C_HEADS)[None, :],
        wa=w_out[:A_V], wb=w_out[A_V:A_V + B_W], wc=w_out[A_V + B_W:],
        g2=p['norm2_g'][l].astype(f32)[None, :],
        wv=chunks(w_up[:, :D_FF]), wg=chunks(w_up[:, D_FF:]),
        wd=p['w_down'][l].astype(bf16).reshape(nC, FF_CHUNK, D_MODEL),
        cp=jnp.transpose(cp.reshape(8, nC, FF_CHUNK), (1, 0, 2)),
        lam_init=0.8 - 0.6 * math.exp(-0.3 * l),
    )


def _layer(x, mem, lp, shared):
    qa, ka, va, qb, kb, vb, qc = _proj(x, lp['g1'], lp['w_in'], shared['gmat'], lp['gain'])
    kc, vc = _mem_kv(mem, lp['mem_g'], lp['w_mem'], shared['gmat'], lp['gain_c'])
    oa = _attn_a(qa, ka, va, shared['btiles'], shared['cfar'], lp['lamv'], lp['sg'], lp['lam_init'])
    ob, oc = _local(qb, kb, vb, lp['na'], qc, kc, vc)
    x1 = _mix(x, oa, ob, oc, lp['wa'], lp['wb'], lp['wc'])
    return _ffn(x1, lp['g2'], lp['wv'], lp['wg'], lp['wd'], lp['cp'])


def kernel(x_prompt, x_sample, mem_prompt, mem_sample, norm1_g, w_in, qn_a, kn_a, lam_q1, lam_k1, lam_q2, lam_k2, subln_g, rel_bias, qn_b, kn_b, na_bias, mem_g, w_mem_kv, qn_c, kn_c, w_out, norm2_g, w_up, conv_w, conv_b, w_down):
    p = dict(norm1_g=norm1_g, w_in=w_in, qn_a=qn_a, kn_a=kn_a, lam_q1=lam_q1, lam_k1=lam_k1, lam_q2=lam_q2,
             lam_k2=lam_k2, subln_g=subln_g, qn_b=qn_b, kn_b=kn_b, na_bias=na_bias, mem_g=mem_g,
             w_mem_kv=w_mem_kv, qn_c=qn_c, kn_c=kn_c, w_out=w_out, norm2_g=norm2_g, w_up=w_up,
             conv_w=conv_w, conv_b=conv_b, w_down=w_down)
    depth = w_in.shape[0]
    group = np.arange(LANE_BLOCK) // HEAD_DIM
    gmat = jnp.asarray((group[:, None] == group[None, :]) / HEAD_DIM, bf16)
    btiles, cfar = _t5_tiles(rel_bias, ATT_T)
    shared = dict(gmat=gmat, btiles=btiles, cfar=cfar)
    layers = [_layer_params(l, p) for l in range(depth)]
    y_prompt, y_sample = x_prompt, x_sample
    for lp in layers:
        y_prompt = _layer(y_prompt, mem_prompt, lp, shared)
    for lp in layers:
        y_sample = _layer(y_sample, mem_sample, lp, shared)
    return (y_prompt, y_sample)
```

```python
import functools
import math

import numpy as np
import jax
import jax.numpy as jnp
from jax import lax
from jax.experimental import pallas as pl
from jax.experimental.pallas import tpu as pltpu

D_MODEL = 1024
HEAD_DIM = 64
A_HEADS = 4
A_VDIM = 2 * HEAD_DIM
B_HEADS = 4
C_HEADS = 4
MEM_TOKENS = 256
GRID_W = 64
NA_ROWS = 8
NA_COLS = 16
T5_BUCKETS = 32
T5_MAX_DIST = 128
D_FF = 2816
CONV_WIDTH = 3
EPS = 1e-6

A_QK = A_HEADS * HEAD_DIM
A_V = A_HEADS * A_VDIM
B_W = B_HEADS * HEAD_DIM
C_W = C_HEADS * HEAD_DIM
IN_WIDTH = 4 * A_QK + A_V + 3 * B_W + C_W

LANE_BLOCK = 256
NEG = -1e30
LOG2E = math.log2(math.e)
FAR_LANES = 32
VMEM_LIMIT = 56 * 1024 * 1024

PROJ_ROWS = 512
ATT_T = 512
PIPE_UNROLL = 4
LOCAL_ROWS = 16
LOCAL_UNROLL = 4
MEM_Q_TILE = 256
MIX_ROWS = 512
FFN_ROWS = 512
FF_CHUNK = 256
HALO = 8

f32 = jnp.float32
bf16 = jnp.bfloat16


def _rms(x, g):
    ms = jnp.mean(x * x, axis=-1, keepdims=True)
    return x * lax.rsqrt(ms + EPS) * g


def _group_normed(z, gmat, gain):
    msq = jnp.dot((z * z).astype(bf16), gmat, preferred_element_type=f32)
    return z * lax.rsqrt(msq + EPS) * gain


def _proj_kernel(x_ref, g1_ref, w_ref, gmat_ref, gain_ref,
                 qa_ref, ka_ref, va_ref, qb_ref, kb_ref, vb_ref, qc_ref):
    h = _rms(x_ref[0], g1_ref[...]).astype(bf16)
    gmat = gmat_ref[...]

    def block(n, normed):
        c0 = n * LANE_BLOCK
        z = jnp.dot(h, w_ref[:, c0:c0 + LANE_BLOCK], preferred_element_type=f32)
        if normed:
            z = _group_normed(z, gmat, gain_ref[:, c0:c0 + LANE_BLOCK])
        return z.astype(bf16)

    for pair in range(2):
        for ref, n, normed in ((qa_ref, pair, True), (ka_ref, 2 + pair, True), (va_ref, 4 + pair, False)):
            z = block(n, normed)
            ref[0, 2 * pair] = z[:, :128]
            ref[0, 2 * pair + 1] = z[:, 128:]
    qb_ref[0] = block(6, True)
    kb_ref[0] = block(7, True)
    vb_ref[0] = block(8, False)
    qc_ref[0] = block(9, True)


def _proj(x, g1, w, gmat, gain):
    B, S, _ = x.shape
    tm = PROJ_ROWS
    const = lambda b, i: (0, 0)
    head_spec = pl.BlockSpec((1, A_HEADS, tm, 128), lambda b, i: (b, 0, i, 0))
    tok_spec = pl.BlockSpec((1, tm, LANE_BLOCK), lambda b, i: (b, i, 0))
    head_shape = jax.ShapeDtypeStruct((B, A_HEADS, S, 128), bf16)
    tok_shape = jax.ShapeDtypeStruct((B, S, LANE_BLOCK), bf16)
    return pl.pallas_call(
        _proj_kernel,
        grid=(B, S // tm),
        in_specs=[
            pl.BlockSpec((1, tm, D_MODEL), lambda b, i: (b, i, 0)),
            pl.BlockSpec((1, D_MODEL), const),
            pl.BlockSpec((D_MODEL, IN_WIDTH), const),
            pl.BlockSpec((LANE_BLOCK, LANE_BLOCK), const),
            pl.BlockSpec((1, IN_WIDTH), const),
        ],
        out_specs=[head_spec, head_spec, head_spec, tok_spec, tok_spec, tok_spec, tok_spec],
        out_shape=[head_shape, head_shape, head_shape, tok_shape, tok_shape, tok_shape, tok_shape],
        compiler_params=pltpu.CompilerParams(
            dimension_semantics=("parallel", "parallel"), vmem_limit_bytes=VMEM_LIMIT),
        name="proj",
    )(x, g1, w, gmat, gain)


def _mem_kernel(m_ref, g_ref, w_ref, gmat_ref, gain_ref, kc_ref, vc_ref):
    h = _rms(m_ref[0], g_ref[...]).astype(bf16)
    z = jnp.dot(h, w_ref[...], preferred_element_type=f32)
    kc_ref[0] = _group_normed(z[:, :C_W], gmat_ref[...], gain_ref[...]).astype(bf16)
    vc_ref[0] = z[:, C_W:].astype(bf16)


def _mem_kv(mem, g, w, gmat, gain):
    B, M, _ = mem.shape
    const = lambda b: (0, 0)
    spec = pl.BlockSpec((1, M, C_W), lambda b: (b, 0, 0))
    shape = jax.ShapeDtypeStruct((B, M, C_W), bf16)
    return pl.pallas_call(
        _mem_kernel,
        grid=(B,),
        in_specs=[
            pl.BlockSpec((1, M, D_MODEL), lambda b: (b, 0, 0)),
            pl.BlockSpec((1, D_MODEL), const),
            pl.BlockSpec((D_MODEL, 2 * C_W), const),
            pl.BlockSpec((LANE_BLOCK, LANE_BLOCK), const),
            pl.BlockSpec((1, C_W), const),
        ],
        out_specs=[spec, spec],
        out_shape=[shape, shape],
        compiler_params=pltpu.CompilerParams(dimension_semantics=("parallel",)),
        name="mem_kv",
    )(mem, g, w, gmat, gain)


def _attn_a_kernel(cfar_ref, q_ref, k_ref, v_ref, bt_ref, lam_ref, sg_ref, o_ref,
                   kext_ref, vext_ref, qs_ref, sa_ref, sb_ref, acc_ref, m_ref, *, T, S, lam_init):
    nk = S // T
    nc = T // 128
    h = pl.program_id(1)
    i = pl.program_id(2)

    @pl.when(i == 0)
    def _():
        row = lax.broadcasted_iota(jnp.int32, (S, 128), 0)
        lane = lax.broadcasted_iota(jnp.int32, (S, 128), 1)
        kext_ref[:, :128] = k_ref[0, 0]
        onehot = (lane < 2 * FAR_LANES) & ((lane & (FAR_LANES - 1)) == row // T)
        kext_ref[:, 128:] = jnp.where(onehot, 1.0, 0.0).astype(bf16)
        vext_ref[:, :A_VDIM] = v_ref[0, 0]
        vext_ref[:, A_VDIM:] = jnp.where(lane == 0, 1.0, 0.0).astype(bf16)

    q = q_ref[0, 0]
    lane = lax.broadcasted_iota(jnp.int32, (T, 128), 1)
    zero = jnp.zeros_like(q)
    qs_ref[:T, :128] = jnp.where(lane < HEAD_DIM, q, zero)
    qs_ref[T:, :128] = jnp.where(lane >= HEAD_DIM, q, zero)
    chunk = lane & (FAR_LANES - 1)
    c = jnp.where(chunk <= i - 2, cfar_ref[h, 0], jnp.where(chunk >= i + 2, cfar_ref[h, 1], 0.0))
    c_hi = c.astype(bf16)
    c_lo = (c - c_hi.astype(f32)).astype(bf16)
    ext = jnp.where(lane < FAR_LANES, c_hi, jnp.where(lane < 2 * FAR_LANES, c_lo, zero))
    qs_ref[:T, 128:] = ext
    qs_ref[T:, 128:] = ext
    acc_ref[...] = jnp.zeros_like(acc_ref)
    m_ref[...] = jnp.full_like(m_ref, NEG)

    def scores(j):
        k0 = pl.multiple_of(j * T, T)
        return lax.dot_general(qs_ref[...], kext_ref[pl.ds(k0, T), :],
                               (((1,), (1,)), ((), ())), preferred_element_type=f32)

    def softmax_pv(j, s_ref, with_tile):
        k0 = pl.multiple_of(j * T, T)
        cols = [s_ref[:, n * 128:(n + 1) * 128] for n in range(nc)]
        if with_tile:
            near = jnp.abs(j - i) <= 1
            tile = jnp.where(near, j - i + 1, 3)
            cols = [(col.reshape(2, T, 128) + bt_ref[0, tile, :, n * 128:(n + 1) * 128][None]).reshape(2 * T, 128)
                    for n, col in enumerate(cols)]
        mx = cols[0]
        for n in range(1, nc):
            mx = jnp.maximum(mx, cols[n])
        m_prev = m_ref[...]
        m_new = jnp.maximum(m_prev, jnp.broadcast_to(jnp.max(mx, axis=-1, keepdims=True), m_prev.shape))
        alpha = jnp.exp2(m_prev - m_new)
        p = jnp.concatenate([jnp.exp2(col - m_new).astype(bf16) for col in cols], axis=1)
        pv = jnp.dot(p, vext_ref[pl.ds(k0, T), :], preferred_element_type=f32)
        acc_ref[:, :128] = alpha * acc_ref[:, :128] + pv[:, :128]
        acc_ref[:, 128:] = alpha * acc_ref[:, 128:] + pv[:, 128:]
        m_ref[...] = m_new

    lo = jnp.maximum(i - 1, 0)
    n_near = jnp.minimum(i + 2, nk) - lo
    n_far = nk - n_near

    def chunk_at(t):
        return jnp.where(t < n_far, jnp.where(t < lo, t, t + n_near), lo + t - n_far)

    bufs = (sa_ref, sb_ref)
    sa_ref[...] = scores(chunk_at(0))

    def block(t0, n, tail):
        for u in range(n):
            if not (tail and u == n - 1):
                bufs[(u + 1) % 2][...] = scores(chunk_at(t0 + u + 1))
            softmax_pv(chunk_at(t0 + u), bufs[u % 2], tail and u >= n - 3)

    def main(g, carry):
        block(g * PIPE_UNROLL, PIPE_UNROLL, False)
        return carry

    lax.fori_loop(0, (nk - PIPE_UNROLL) // PIPE_UNROLL, main, 0)
    block(nk - PIPE_UNROLL, PIPE_UNROLL, True)

    acc = acc_ref[...]
    o1 = acc[:T, :A_VDIM]
    l1 = acc[:T, A_VDIM:A_VDIM + 1]
    o2 = acc[T:, :A_VDIM]
    l2 = acc[T:, A_VDIM:A_VDIM + 1]
    lam = (jnp.exp(jnp.sum(lam_ref[0:1] * lam_ref[1:2], axis=-1, keepdims=True))
           - jnp.exp(jnp.sum(lam_ref[2:3] * lam_ref[3:4], axis=-1, keepdims=True)) + lam_init)
    o = o1 / l1 - lam * (o2 / l2)
    o_ref[0] = (_rms(o, sg_ref[...]) * (1.0 - lam_init)).astype(bf16)


def _attn_a(qa, ka, va, btiles, cfar, lamv, sg, lam_init):
    B, H, S, _ = qa.shape
    T = ATT_T
    assert S % (PIPE_UNROLL * T) == 0 and S // T <= FAR_LANES
    kernel = functools.partial(_attn_a_kernel, T=T, S=S, lam_init=lam_init)
    return pl.pallas_call(
        kernel,
        grid=(B, H, S // T),
        in_specs=[
            pl.BlockSpec(memory_space=pltpu.SMEM),
            pl.BlockSpec((1, 1, T, 128), lambda b, h, i: (b, h, i, 0)),
            pl.BlockSpec((1, 1, S, 128), lambda b, h, i: (b, h, 0, 0)),
            pl.BlockSpec((1, 1, S, 128), lambda b, h, i: (b, h, 0, 0)),
            pl.BlockSpec((1, 4, T, T), lambda b, h, i: (h, 0, 0, 0)),
            pl.BlockSpec((4, HEAD_DIM), lambda b, h, i: (0, 0)),
            pl.BlockSpec((1, A_VDIM), lambda b, h, i: (0, 0)),
        ],
        out_specs=pl.BlockSpec((1, T, A_VDIM), lambda b, h, i: (b, i, h)),
        out_shape=jax.ShapeDtypeStruct((B, S, A_V), bf16),
        scratch_shapes=[
            pltpu.VMEM((S, 256), bf16),
            pltpu.VMEM((S, 2 * A_VDIM), bf16),
            pltpu.VMEM((2 * T, 256), bf16),
            pltpu.VMEM((2 * T, T), f32),
            pltpu.VMEM((2 * T, T), f32),
            pltpu.VMEM((2 * T, 2 * A_VDIM), f32),
            pltpu.VMEM((2 * T, 128), f32),
        ],
        compiler_params=pltpu.CompilerParams(
            dimension_semantics=("arbitrary", "arbitrary", "arbitrary"), vmem_limit_bytes=VMEM_LIMIT),
        name="attn_a",
    )(cfar, qa, ka, va, btiles, lamv, sg)


def _head_rows_attention(q, k, v, bias, nh):
    n = q.shape[0]
    rows = nh * n
    row_head = lax.broadcasted_iota(jnp.int32, (rows, LANE_BLOCK), 0) // n
    lane_head = lax.broadcasted_iota(jnp.int32, (rows, LANE_BLOCK), 1) // HEAD_DIM
    own = row_head == lane_head
    qs = jnp.where(own, jnp.concatenate([q] * nh, axis=0), jnp.zeros((rows, LANE_BLOCK), q.dtype))
    s = lax.dot_general(qs, k, (((1,), (1,)), ((), ())), preferred_element_type=f32)
    if bias is not None:
        s = s + bias
    m = jnp.max(s, axis=-1, keepdims=True)
    e = jnp.exp(s - m)
    l = jnp.sum(e, axis=-1, keepdims=True)
    r = jnp.dot(e.astype(bf16), v, preferred_element_type=f32) / l
    r = jnp.where(own, r, 0.0)
    out = r[:n]
    for hh in range(1, nh):
        out = out + r[hh * n:(hh + 1) * n]
    return out


def _local_kernel(qb_ref, kb_ref, vb_ref, bt_ref, qc_ref, kc_ref, vc_ref, ob_ref, oc_ref, *, R):
    g = pl.program_id(1)
    kh = NA_ROWS

    def row(rr, c):
        r = g * LOCAL_ROWS + rr
        rs = jnp.clip(r - kh // 2, 0, R - kh)
        q0 = pl.multiple_of(rr * GRID_W, GRID_W)
        k0 = pl.multiple_of(rs * GRID_W, GRID_W)
        q = qb_ref[0, pl.ds(q0, GRID_W), :]
        k = kb_ref[0, pl.ds(k0, kh * GRID_W), :]
        v = vb_ref[0, pl.ds(k0, kh * GRID_W), :]
        o = _head_rows_attention(q, k, v, bt_ref[r - rs], B_HEADS)
        ob_ref[0, pl.ds(q0, GRID_W), :] = o.astype(bf16)
        return c

    lax.fori_loop(0, LOCAL_ROWS, row, 0, unroll=LOCAL_UNROLL)

    kc = kc_ref[0]
    vc = vc_ref[0]

    def mem_tile(t, c):
        q0 = pl.multiple_of(t * MEM_Q_TILE, MEM_Q_TILE)
        o = _head_rows_attention(qc_ref[0, pl.ds(q0, MEM_Q_TILE), :], kc, vc, None, C_HEADS)
        oc_ref[0, pl.ds(q0, MEM_Q_TILE), :] = o.astype(bf16)
        return c

    lax.fori_loop(0, LOCAL_ROWS * GRID_W // MEM_Q_TILE, mem_tile, 0, unroll=2)


def _local(qb, kb, vb, na_tiles, qc, kc, vc):
    B, S, _ = qb.shape
    R = S // GRID_W
    rows = LOCAL_ROWS * GRID_W
    tile = pl.BlockSpec((1, rows, LANE_BLOCK), lambda b, g: (b, g, 0))
    full = pl.BlockSpec((1, S, LANE_BLOCK), lambda b, g: (b, 0, 0))
    memspec = pl.BlockSpec((1, MEM_TOKENS, LANE_BLOCK), lambda b, g: (b, 0, 0))
    shape = jax.ShapeDtypeStruct((B, S, LANE_BLOCK), bf16)
    return pl.pallas_call(
        functools.partial(_local_kernel, R=R),
        grid=(B, R // LOCAL_ROWS),
        in_specs=[tile, full, full,
                  pl.BlockSpec(na_tiles.shape, lambda b, g: (0, 0, 0)),
                  tile, memspec, memspec],
        out_specs=[tile, tile],
        out_shape=[shape, shape],
        compiler_params=pltpu.CompilerParams(
            dimension_semantics=("parallel", "parallel"), vmem_limit_bytes=VMEM_LIMIT),
        name="local",
    )(qb, kb, vb, na_tiles, qc, kc, vc)


def _mix_kernel(x_ref, oa_ref, ob_ref, oc_ref, wa_ref, wb_ref, wc_ref, o_ref):
    acc = jnp.dot(oa_ref[0], wa_ref[...], preferred_element_type=f32)
    acc = acc + jnp.dot(ob_ref[0], wb_ref[...], preferred_element_type=f32)
    acc = acc + jnp.dot(oc_ref[0], wc_ref[...], preferred_element_type=f32)
    o_ref[0] = x_ref[0] + acc


def _mix(x, oa, ob, oc, wa, wb, wc):
    B, S, _ = x.shape
    tm = MIX_ROWS
    const = lambda b, i: (0, 0)
    row = lambda w: pl.BlockSpec((1, tm, w), lambda b, i: (b, i, 0))
    return pl.pallas_call(
        _mix_kernel,
        grid=(B, S // tm),
        in_specs=[row(D_MODEL), row(A_V), row(B_W), row(C_W),
                  pl.BlockSpec((A_V, D_MODEL), const),
                  pl.BlockSpec((B_W, D_MODEL), const),
                  pl.BlockSpec((C_W, D_MODEL), const)],
        out_specs=row(D_MODEL),
        out_shape=jax.ShapeDtypeStruct((B, S, D_MODEL), f32),
        compiler_params=pltpu.CompilerParams(
            dimension_semantics=("parallel", "parallel"), vmem_limit_bytes=VMEM_LIMIT),
        name="mix",
    )(x, oa, ob, oc, wa, wb, wc)


def _ffn_kernel(x_ref, xp_ref, xn_ref, g2_ref, wv_ref, wg_ref, wd_ref, cp_ref, o_ref,
                h_ref, acc_ref, uv_ref, ug_ref, *, T, nT, nC):
    i = pl.program_id(1)
    g2 = g2_ref[...]
    x = x_ref[0]
    hp = jnp.where(i == 0, 0.0, _rms(xp_ref[0], g2))
    hn = jnp.where(i == nT - 1, 0.0, _rms(xn_ref[0], g2))
    h_ref[...] = jnp.concatenate([hp, _rms(x, g2), hn], axis=0).astype(bf16)
    acc_ref[...] = jnp.zeros_like(acc_ref)

    def conv(u_ref, cp, r0, rb):
        return (cp[rb:rb + 1] + cp[r0:r0 + 1] * u_ref[HALO - 1:HALO - 1 + T]
                + cp[r0 + 1:r0 + 2] * u_ref[HALO:HALO + T] + cp[r0 + 2:r0 + 3] * u_ref[HALO + 1:HALO + 1 + T])

    def up(c, slot):
        h = h_ref[...]
        uv_ref[slot] = jnp.dot(h, wv_ref[c], preferred_element_type=f32)
        ug_ref[slot] = jnp.dot(h, wg_ref[c], preferred_element_type=f32)

    def down(c, slot):
        cp = cp_ref[c]
        val = conv(uv_ref.at[slot], cp, 0, 6)
        gate = conv(ug_ref.at[slot], cp, 3, 7)
        act = (gate * jax.nn.sigmoid(gate) * val).astype(bf16)
        return jnp.dot(act, wd_ref[c], preferred_element_type=f32)

    up(0, 0)
    for c in range(nC):
        if c + 1 < nC:
            up(c + 1, (c + 1) % 2)
        acc_ref[...] += down(c, c % 2)
    o_ref[0] = x + acc_ref[...]


def _ffn(x, g2, wv, wg, wd, cp):
    B, S, _ = x.shape
    T = FFN_ROWS
    nT = S // T
    nC = wv.shape[0]
    hb = T // HALO
    const2 = lambda b, i: (0, 0)
    const3 = lambda b, i: (0, 0, 0)
    single = dict(pipeline_mode=pl.Buffered(1))
    return pl.pallas_call(
        functools.partial(_ffn_kernel, T=T, nT=nT, nC=nC),
        grid=(B, nT),
        in_specs=[
            pl.BlockSpec((1, T, D_MODEL), lambda b, i: (b, i, 0)),
            pl.BlockSpec((1, HALO, D_MODEL), lambda b, i: (b, jnp.maximum(i * hb - 1, 0), 0)),
            pl.BlockSpec((1, HALO, D_MODEL), lambda b, i: (b, jnp.minimum((i + 1) * hb, S // HALO - 1), 0)),
            pl.BlockSpec((1, D_MODEL), const2),
            pl.BlockSpec(wv.shape, const3, **single),
            pl.BlockSpec(wg.shape, const3, **single),
            pl.BlockSpec(wd.shape, const3, **single),
            pl.BlockSpec(cp.shape, const3),
        ],
        out_specs=pl.BlockSpec((1, T, D_MODEL), lambda b, i: (b, i, 0)),
        out_shape=jax.ShapeDtypeStruct((B, S, D_MODEL), f32),
        scratch_shapes=[pltpu.VMEM((T + 2 * HALO, D_MODEL), bf16), pltpu.VMEM((T, D_MODEL), f32),
                        pltpu.VMEM((2, T + 2 * HALO, FF_CHUNK), f32),
                        pltpu.VMEM((2, T + 2 * HALO, FF_CHUNK), f32)],
        compiler_params=pltpu.CompilerParams(
            dimension_semantics=("parallel", "parallel"), vmem_limit_bytes=VMEM_LIMIT),
        name="ffn",
    )(x, x, x, g2, wv, wg, wd, cp)


def _t5_bucket(rp):
    half = T5_BUCKETS // 2
    max_exact = half // 2
    ret = jnp.where(rp > 0, half, 0)
    n = jnp.abs(rp)
    nf = jnp.maximum(n, 1).astype(f32)
    large = max_exact + (jnp.log(nf / max_exact) / math.log(T5_MAX_DIST / max_exact)
                         * (half - max_exact)).astype(jnp.int32)
    large = jnp.minimum(large, half - 1)
    return ret + jnp.where(n < max_exact, n, large)


def _t5_tiles(rel_bias, T):
    assert T >= T5_MAX_DIST
    off = np.arange(T)[None, :] - np.arange(T)[:, None]
    rp = np.stack([off - T, off, off + T]).astype(np.int32)
    bucket = _t5_bucket(jnp.asarray(rp))[None]
    table = jnp.transpose(rel_bias).astype(f32) * LOG2E
    tiles = jnp.zeros((table.shape[0], 3, T, T), f32)
    for b in range(T5_BUCKETS):
        tiles = jnp.where(bucket == b, table[:, b][:, None, None, None], tiles)
    tiles = jnp.concatenate([tiles, jnp.zeros_like(tiles[:, :1])], axis=1)
    far_bucket = _t5_bucket(jnp.asarray([-2 * T, 2 * T], jnp.int32))
    far = jnp.zeros((table.shape[0], 2), f32)
    for b in range(T5_BUCKETS):
        far = jnp.where(far_bucket[None, :] == b, table[:, b][:, None], far)
    return tiles, far


def _na_tiles(na_bias_l):
    kh = NA_ROWS
    c = np.arange(GRID_W)
    cs = np.clip(c - NA_COLS // 2, 0, GRID_W - NA_COLS)
    kc = np.arange(GRID_W)
    valid = (kc[None, :] >= cs[:, None]) & (kc[None, :] < cs[:, None] + NA_COLS)
    dcol = kc[None, :] - c[:, None] + (NA_COLS - 1)
    delta = np.arange(kh)
    ki = np.arange(kh)
    drow = ki[None, :] - delta[:, None] + (NA_ROWS - 1)
    row_sel = (drow[:, :, None] == np.arange(2 * NA_ROWS - 1)).astype(np.float32)
    col_sel = (dcol[None] == np.arange(2 * NA_COLS - 1)[:, None, None]).astype(np.float32)
    hp = lax.Precision.HIGHEST
    tmp = jnp.einsum('hab,bck->hack', na_bias_l.astype(f32), col_sel, precision=hp)
    vals = jnp.einsum('dia,hack->dhcik', row_sel, tmp, precision=hp)
    vals = jnp.where(valid[None, None, :, None, :], vals, NEG)
    return vals.reshape(kh, B_HEADS * GRID_W, kh * GRID_W)


def _in_perm():
    q1, q2, k1, k2 = 0, A_QK, 2 * A_QK, 3 * A_QK
    cols = []
    for a, b in ((q1, q2), (k1, k2)):
        for h in range(A_HEADS):
            cols += list(range(a + h * HEAD_DIM, a + (h + 1) * HEAD_DIM))
            cols += list(range(b + h * HEAD_DIM, b + (h + 1) * HEAD_DIM))
    cols += list(range(4 * A_QK, IN_WIDTH))
    return np.asarray(cols, np.int32)


def _layer_params(l, p):
    scale = HEAD_DIM ** -0.5
    ones = lambda n: jnp.ones((n,), f32)
    tile = lambda g, n: jnp.tile(g.astype(f32), n)
    gain = jnp.concatenate([
        tile(p['qn_a'][l], 2 * A_HEADS) * (scale * LOG2E), tile(p['kn_a'][l], 2 * A_HEADS), ones(A_V),
        tile(p['qn_b'][l], B_HEADS) * scale, tile(p['kn_b'][l], B_HEADS), ones(B_W),
        tile(p['qn_c'][l], C_HEADS) * scale])[None, :]
    nC = D_FF // FF_CHUNK
    w_up = p['w_up'][l].astype(bf16)
    chunks = lambda w: jnp.transpose(w.reshape(D_MODEL, nC, FF_CHUNK), (1, 0, 2))
    cw = p['conv_w'][l].astype(f32)
    cb = p['conv_b'][l].astype(f32)
    cp = jnp.concatenate([cw[:, :D_FF], cw[:, D_FF:], cb[None, :D_FF], cb[None, D_FF:]], axis=0)
    w_out = p['w_out'][l].astype(bf16)
    return dict(
        g1=p['norm1_g'][l].astype(f32)[None, :],
        w_in=p['w_in'][l][:, _in_perm()].astype(bf16),
        gain=gain,
        lamv=jnp.stack([p['lam_q1'][l], p['lam_k1'][l], p['lam_q2'][l], p['lam_k2'][l]]).astype(f32),
        sg=p['subln_g'][l].astype(f32)[None, :],
        na=_na_tiles(p['na_bias'][l]),
        mem_g=p['mem_g'][l].astype(f32)[None, :],
        w_mem=p['w_mem_kv'][l].astype(bf16),
        gain_c=tile(p['kn_c'][l], C_HEADS)[None, :],
        wa=w_out[:A_V], wb=w_out[A_V:A_V + B_W], wc=w_out[A_V + B_W:],
        g2=p['norm2_g'][l].astype(f32)[None, :],
        wv=chunks(w_up[:, :D_FF]), wg=chunks(w_up[:, D_FF:]),
        wd=p['w_down'][l].astype(bf16).reshape(nC, FF_CHUNK, D_MODEL),
        cp=jnp.transpose(cp.reshape(8, nC, FF_CHUNK), (1, 0, 2)),
        lam_init=0.8 - 0.6 * math.exp(-0.3 * l),
    )


def _layer(x, mem, lp, shared):
    qa, ka, va, qb, kb, vb, qc = _proj(x, lp['g1'], lp['w_in'], shared['gmat'], lp['gain'])
    kc, vc = _mem_kv(mem, lp['mem_g'], lp['w_mem'], shared['gmat'], lp['gain_c'])
    oa = _attn_a(qa, ka, va, shared['btiles'], shared['cfar'], lp['lamv'], lp['sg'], lp['lam_init'])
    ob, oc = _local(qb, kb, vb, lp['na'], qc, kc, vc)
    x1 = _mix(x, oa, ob, oc, lp['wa'], lp['wb'], lp['wc'])
    return _ffn(x1, lp['g2'], lp['wv'], lp['wg'], lp['wd'], lp['cp'])


def kernel(x_prompt, x_sample, mem_prompt, mem_sample, norm1_g, w_in, qn_a, kn_a, lam_q1, lam_k1, lam_q2, lam_k2, subln_g, rel_bias, qn_b, kn_b, na_bias, mem_g, w_mem_kv, qn_c, kn_c, w_out, norm2_g, w_up, conv_w, conv_b, w_down):
    p = dict(norm1_g=norm1_g, w_in=w_in, qn_a=qn_a, kn_a=kn_a, lam_q1=lam_q1, lam_k1=lam_k1, lam_q2=lam_q2,
             lam_k2=lam_k2, subln_g=subln_g, qn_b=qn_b, kn_b=kn_b, na_bias=na_bias, mem_g=mem_g,
             w_mem_kv=w_mem_kv, qn_c=qn_c, kn_c=kn_c, w_out=w_out, norm2_g=norm2_g, w_up=w_up,
             conv_w=conv_w, conv_b=conv_b, w_down=w_down)
    depth = w_in.shape[0]
    group = np.arange(LANE_BLOCK) // HEAD_DIM
    gmat = jnp.asarray((group[:, None] == group[None, :]) / HEAD_DIM, bf16)
    btiles, cfar = _t5_tiles(rel_bias, ATT_T)
    shared = dict(gmat=gmat, btiles=btiles, cfar=cfar)
    layers = [_layer_params(l, p) for l in range(depth)]
    y_prompt, y_sample = x_prompt, x_sample
    for lp in layers:
        y_prompt = _layer(y_prompt, mem_prompt, lp, shared)
    for lp in layers:
        y_sample = _layer(y_sample, mem_sample, lp, shared)
    return (y_prompt, y_sample)
```

```python
import functools
import math

import numpy as np
import jax
import jax.numpy as jnp
from jax import lax
from jax.experimental import pallas as pl
from jax.experimental.pallas import tpu as pltpu

D_MODEL = 1024
HEAD_DIM = 64
A_HEADS = 4
A_VDIM = 2 * HEAD_DIM
B_HEADS = 4
C_HEADS = 4
MEM_TOKENS = 256
GRID_W = 64
NA_ROWS = 8
NA_COLS = 16
T5_BUCKETS = 32
T5_MAX_DIST = 128
D_FF = 2816
CONV_WIDTH = 3
EPS = 1e-6

A_QK = A_HEADS * HEAD_DIM
A_V = A_HEADS * A_VDIM
B_W = B_HEADS * HEAD_DIM
C_W = C_HEADS * HEAD_DIM
IN_WIDTH = 4 * A_QK + A_V + 3 * B_W + C_W
MIX_WIDTH = A_V + B_W + C_W

LANE_BLOCK = 256
NEG = -1e30
LOG2E = math.log2(math.e)
FAR_LANES = 32
VMEM_LIMIT = 56 * 1024 * 1024

PROJ_ROWS = 1024
ATT_T = 512
PIPE_UNROLL = 4
PIPE_FLAT_MAX = 4
LOCAL_ROWS = 16
LOCAL_UNROLL = 4
MEM_Q_TILE = 256
FFN_ROWS = 512
FF_CHUNK = 256
HALO = 8
MIX_HALO = 16

f32 = jnp.float32
bf16 = jnp.bfloat16


def _rms(x, g):
    ms = jnp.mean(x * x, axis=-1, keepdims=True)
    return x * lax.rsqrt(ms + EPS) * g


def _group_normed(z, gmat, gain):
    msq = jnp.dot((z * z).astype(bf16), gmat, preferred_element_type=f32)
    return z * lax.rsqrt(msq + EPS) * gain


def _proj_kernel(x_ref, g1_ref, w_ref, gmat_ref, gain_ref,
                 qa_ref, ka_ref, va_ref, qb_ref, kb_ref, vb_ref, qc_ref):
    h = _rms(x_ref[0], g1_ref[...]).astype(bf16)
    gmat = gmat_ref[...]

    def block(n, normed):
        c0 = n * LANE_BLOCK
        z = jnp.dot(h, w_ref[:, c0:c0 + LANE_BLOCK], preferred_element_type=f32)
        if normed:
            z = _group_normed(z, gmat, gain_ref[:, c0:c0 + LANE_BLOCK])
        return z.astype(bf16)

    for pair in range(2):
        for ref, n, normed in ((qa_ref, pair, True), (ka_ref, 2 + pair, True), (va_ref, 4 + pair, False)):
            z = block(n, normed)
            ref[0, 2 * pair] = z[:, :128]
            ref[0, 2 * pair + 1] = z[:, 128:]
    qb_ref[0] = block(6, True)
    kb_ref[0] = block(7, True)
    vb_ref[0] = block(8, False)
    qc_ref[0] = block(9, True)


def _proj(x, g1, w, gmat, gain):
    B, S, _ = x.shape
    tm = PROJ_ROWS
    const = lambda b, i: (0, 0)
    head_spec = pl.BlockSpec((1, A_HEADS, tm, 128), lambda b, i: (b, 0, i, 0))
    tok_spec = pl.BlockSpec((1, tm, LANE_BLOCK), lambda b, i: (b, i, 0))
    head_shape = jax.ShapeDtypeStruct((B, A_HEADS, S, 128), bf16)
    tok_shape = jax.ShapeDtypeStruct((B, S, LANE_BLOCK), bf16)
    return pl.pallas_call(
        _proj_kernel,
        grid=(B, S // tm),
        in_specs=[
            pl.BlockSpec((1, tm, D_MODEL), lambda b, i: (b, i, 0)),
            pl.BlockSpec((1, D_MODEL), const),
            pl.BlockSpec((D_MODEL, IN_WIDTH), const),
            pl.BlockSpec((LANE_BLOCK, LANE_BLOCK), const),
            pl.BlockSpec((1, IN_WIDTH), const),
        ],
        out_specs=[head_spec, head_spec, head_spec, tok_spec, tok_spec, tok_spec, tok_spec],
        out_shape=[head_shape, head_shape, head_shape, tok_shape, tok_shape, tok_shape, tok_shape],
        compiler_params=pltpu.CompilerParams(
            dimension_semantics=("parallel", "parallel"), vmem_limit_bytes=VMEM_LIMIT),
        name="proj",
    )(x, g1, w, gmat, gain)


def _mem_kernel(m_ref, g_ref, w_ref, gmat_ref, gain_ref, kc_ref, vc_ref):
    h = _rms(m_ref[0], g_ref[...]).astype(bf16)
    z = jnp.dot(h, w_ref[...], preferred_element_type=f32)
    kc_ref[0] = _group_normed(z[:, :C_W], gmat_ref[...], gain_ref[...]).astype(bf16)
    vc_ref[0] = z[:, C_W:].astype(bf16)


def _mem_kv(mem, g, w, gmat, gain):
    B, M, _ = mem.shape
    const = lambda b: (0, 0)
    spec = pl.BlockSpec((1, M, C_W), lambda b: (b, 0, 0))
    shape = jax.ShapeDtypeStruct((B, M, C_W), bf16)
    return pl.pallas_call(
        _mem_kernel,
        grid=(B,),
        in_specs=[
            pl.BlockSpec((1, M, D_MODEL), lambda b: (b, 0, 0)),
            pl.BlockSpec((1, D_MODEL), const),
            pl.BlockSpec((D_MODEL, 2 * C_W), const),
            pl.BlockSpec((LANE_BLOCK, LANE_BLOCK), const),
            pl.BlockSpec((1, C_W), const),
        ],
        out_specs=[spec, spec],
        out_shape=[shape, shape],
        compiler_params=pltpu.CompilerParams(dimension_semantics=("parallel",)),
        name="mem_kv",
    )(mem, g, w, gmat, gain)


def _attn_a_kernel(cfar_ref, q_ref, k_ref, v_ref, bt_ref, lam_ref, sg_ref, o_ref,
                   kext_ref, vext_ref, qs_ref, sa_ref, sb_ref, acc_ref, m_ref, *, T, S, lam_init):
    nk = S // T
    nc = T // 128
    h = pl.program_id(1)
    i = pl.program_id(2)

    @pl.when(i == 0)
    def _():
        row = lax.broadcasted_iota(jnp.int32, (S, 128), 0)
        lane = lax.broadcasted_iota(jnp.int32, (S, 128), 1)
        kext_ref[:, :128] = k_ref[0, 0]
        onehot = (lane < 2 * FAR_LANES) & ((lane & (FAR_LANES - 1)) == row // T)
        kext_ref[:, 128:] = jnp.where(onehot, 1.0, 0.0).astype(bf16)
        vext_ref[:, :A_VDIM] = v_ref[0, 0]
        vext_ref[:, A_VDIM:] = jnp.where(lane == 0, 1.0, 0.0).astype(bf16)

    q = q_ref[0, 0]
    lane = lax.broadcasted_iota(jnp.int32, (T, 128), 1)
    zero = jnp.zeros_like(q)
    qs_ref[:T, :128] = jnp.where(lane < HEAD_DIM, q, zero)
    qs_ref[T:, :128] = jnp.where(lane >= HEAD_DIM, q, zero)
    chunk = lane & (FAR_LANES - 1)
    c = jnp.where(chunk <= i - 2, cfar_ref[h, 0], jnp.where(chunk >= i + 2, cfar_ref[h, 1], 0.0))
    c_hi = c.astype(bf16)
    c_lo = (c - c_hi.astype(f32)).astype(bf16)
    ext = jnp.where(lane < FAR_LANES, c_hi, jnp.where(lane < 2 * FAR_LANES, c_lo, zero))
    qs_ref[:T, 128:] = ext
    qs_ref[T:, 128:] = ext
    acc_ref[...] = jnp.zeros_like(acc_ref)
    m_ref[...] = jnp.full_like(m_ref, NEG)

    def scores(j):
        k0 = pl.multiple_of(j * T, T)
        return lax.dot_general(qs_ref[...], kext_ref[pl.ds(k0, T), :],
                               (((1,), (1,)), ((), ())), preferred_element_type=f32)

    def softmax_pv(j, s_ref, with_tile):
        k0 = pl.multiple_of(j * T, T)
        cols = [s_ref[:, n * 128:(n + 1) * 128] for n in range(nc)]
        if with_tile:
            near = jnp.abs(j - i) <= 1
            tile = jnp.where(near, j - i + 1, 3)
            cols = [(col.reshape(2, T, 128) + bt_ref[0, tile, :, n * 128:(n + 1) * 128][None]).reshape(2 * T, 128)
                    for n, col in enumerate(cols)]
        mx = cols[0]
        for n in range(1, nc):
            mx = jnp.maximum(mx, cols[n])
        m_prev = m_ref[...]
        m_new = jnp.maximum(m_prev, jnp.broadcast_to(jnp.max(mx, axis=-1, keepdims=True), m_prev.shape))
        alpha = jnp.exp2(m_prev - m_new)
        p = jnp.concatenate([jnp.exp2(col - m_new).astype(bf16) for col in cols], axis=1)
        pv = jnp.dot(p, vext_ref[pl.ds(k0, T), :], preferred_element_type=f32)
        acc_ref[:, :128] = alpha * acc_ref[:, :128] + pv[:, :128]
        acc_ref[:, 128:] = alpha * acc_ref[:, 128:] + pv[:, 128:]
        m_ref[...] = m_new

    lo = jnp.maximum(i - 1, 0)
    n_near = jnp.minimum(i + 2, nk) - lo
    n_far = nk - n_near

    def chunk_at(t):
        return jnp.where(t < n_far, jnp.where(t < lo, t, t + n_near), lo + t - n_far)

    bufs = (sa_ref, sb_ref)
    sa_ref[...] = scores(chunk_at(0))

    def block(t0, n, tail):
        for u in range(n):
            if not (tail and u == n - 1):
                bufs[(u + 1) % 2][...] = scores(chunk_at(t0 + u + 1))
            softmax_pv(chunk_at(t0 + u), bufs[u % 2], tail and u >= n - 3)

    def main(g, carry):
        block(g * PIPE_UNROLL, PIPE_UNROLL, False)
        return carry

    n_loop = 0 if nk <= PIPE_FLAT_MAX else (nk - PIPE_UNROLL) // PIPE_UNROLL
    if n_loop:
        lax.fori_loop(0, n_loop, main, 0)
    block(n_loop * PIPE_UNROLL, nk - n_loop * PIPE_UNROLL, True)

    acc = acc_ref[...]
    o1 = acc[:T, :A_VDIM]
    l1 = acc[:T, A_VDIM:A_VDIM + 1]
    o2 = acc[T:, :A_VDIM]
    l2 = acc[T:, A_VDIM:A_VDIM + 1]
    lam = (jnp.exp(jnp.sum(lam_ref[0:1] * lam_ref[1:2], axis=-1, keepdims=True))
           - jnp.exp(jnp.sum(lam_ref[2:3] * lam_ref[3:4], axis=-1, keepdims=True)) + lam_init)
    o = o1 / l1 - lam * (o2 / l2)
    o_ref[0] = (_rms(o, sg_ref[...]) * (1.0 - lam_init)).astype(bf16)


def _attn_a(qa, ka, va, btiles, cfar, lamv, sg, lam_init):
    B, H, S, _ = qa.shape
    T = ATT_T
    assert S % (PIPE_UNROLL * T) == 0 and S // T <= FAR_LANES
    kernel = functools.partial(_attn_a_kernel, T=T, S=S, lam_init=lam_init)
    return pl.pallas_call(
        kernel,
        grid=(B, H, S // T),
        in_specs=[
            pl.BlockSpec(memory_space=pltpu.SMEM),
            pl.BlockSpec((1, 1, T, 128), lambda b, h, i: (b, h, i, 0)),
            pl.BlockSpec((1, 1, S, 128), lambda b, h, i: (b, h, 0, 0)),
            pl.BlockSpec((1, 1, S, 128), lambda b, h, i: (b, h, 0, 0)),
            pl.BlockSpec((1, 4, T, T), lambda b, h, i: (h, 0, 0, 0)),
            pl.BlockSpec((4, HEAD_DIM), lambda b, h, i: (0, 0)),
            pl.BlockSpec((1, A_VDIM), lambda b, h, i: (0, 0)),
        ],
        out_specs=pl.BlockSpec((1, T, A_VDIM), lambda b, h, i: (b, i, h)),
        out_shape=jax.ShapeDtypeStruct((B, S, MIX_WIDTH), bf16),
        scratch_shapes=[
            pltpu.VMEM((S, 256), bf16),
            pltpu.VMEM((S, 2 * A_VDIM), bf16),
            pltpu.VMEM((2 * T, 256), bf16),
            pltpu.VMEM((2 * T, T), f32),
            pltpu.VMEM((2 * T, T), f32),
            pltpu.VMEM((2 * T, 2 * A_VDIM), f32),
            pltpu.VMEM((2 * T, 128), f32),
        ],
        compiler_params=pltpu.CompilerParams(
            dimension_semantics=("arbitrary", "arbitrary", "arbitrary"), vmem_limit_bytes=VMEM_LIMIT),
        name="attn_a",
    )(cfar, qa, ka, va, btiles, lamv, sg)


def _head_rows_attention(q, k, v, bias, nh):
    n = q.shape[0]
    rows = nh * n
    row_head = lax.broadcasted_iota(jnp.int32, (rows, LANE_BLOCK), 0) // n
    lane_head = lax.broadcasted_iota(jnp.int32, (rows, LANE_BLOCK), 1) // HEAD_DIM
    own = row_head == lane_head
    qs = jnp.where(own, jnp.concatenate([q] * nh, axis=0), jnp.zeros((rows, LANE_BLOCK), q.dtype))
    s = lax.dot_general(qs, k, (((1,), (1,)), ((), ())), preferred_element_type=f32)
    if bias is not None:
        s = s + bias
    m = jnp.max(s, axis=-1, keepdims=True)
    e = jnp.exp(s - m)
    l = jnp.sum(e, axis=-1, keepdims=True)
    r = jnp.dot(e.astype(bf16), v, preferred_element_type=f32) / l
    r = jnp.where(own, r, 0.0)
    out = r[:n]
    for hh in range(1, nh):
        out = out + r[hh * n:(hh + 1) * n]
    return out


def _local_kernel(qb_ref, kb_ref, vb_ref, bt_ref, qc_ref, kc_ref, vc_ref, mix_in_ref, o_ref, *, R):
    del mix_in_ref
    g = pl.program_id(1)
    kh = NA_ROWS

    def row(rr, c):
        r = g * LOCAL_ROWS + rr
        rs = jnp.clip(r - kh // 2, 0, R - kh)
        q0 = pl.multiple_of(rr * GRID_W, GRID_W)
        k0 = pl.multiple_of(rs * GRID_W, GRID_W)
        q = qb_ref[0, pl.ds(q0, GRID_W), :]
        k = kb_ref[0, pl.ds(k0, kh * GRID_W), :]
        v = vb_ref[0, pl.ds(k0, kh * GRID_W), :]
        o = _head_rows_attention(q, k, v, bt_ref[r - rs], B_HEADS)
        o_ref[0, pl.ds(q0, GRID_W), :B_W] = o.astype(bf16)
        return c

    lax.fori_loop(0, LOCAL_ROWS, row, 0, unroll=LOCAL_UNROLL)

    kc = kc_ref[0]
    vc = vc_ref[0]

    def mem_tile(t, c):
        q0 = pl.multiple_of(t * MEM_Q_TILE, MEM_Q_TILE)
        o = _head_rows_attention(qc_ref[0, pl.ds(q0, MEM_Q_TILE), :], kc, vc, None, C_HEADS)
        o_ref[0, pl.ds(q0, MEM_Q_TILE), B_W:] = o.astype(bf16)
        return c

    lax.fori_loop(0, LOCAL_ROWS * GRID_W // MEM_Q_TILE, mem_tile, 0, unroll=2)


def _local(qb, kb, vb, na_tiles, qc, kc, vc, mix):
    B, S, _ = qb.shape
    R = S // GRID_W
    rows = LOCAL_ROWS * GRID_W
    tile = pl.BlockSpec((1, rows, LANE_BLOCK), lambda b, g: (b, g, 0))
    full = pl.BlockSpec((1, S, LANE_BLOCK), lambda b, g: (b, 0, 0))
    memspec = pl.BlockSpec((1, MEM_TOKENS, LANE_BLOCK), lambda b, g: (b, 0, 0))
    return pl.pallas_call(
        functools.partial(_local_kernel, R=R),
        grid=(B, R // LOCAL_ROWS),
        in_specs=[tile, full, full,
                  pl.BlockSpec(na_tiles.shape, lambda b, g: (0, 0, 0)),
                  tile, memspec, memspec,
                  pl.BlockSpec(memory_space=pl.ANY)],
        out_specs=pl.BlockSpec((1, rows, B_W + C_W), lambda b, g: (b, g, A_V // (B_W + C_W))),
        out_shape=jax.ShapeDtypeStruct(mix.shape, mix.dtype),
        input_output_aliases={7: 0},
        compiler_params=pltpu.CompilerParams(
            dimension_semantics=("parallel", "parallel"), vmem_limit_bytes=VMEM_LIMIT),
        name="local",
    )(qb, kb, vb, na_tiles, qc, kc, vc, mix)


def _ffn_kernel(x_ref, xp_ref, xn_ref, m_ref, mp_ref, mn_ref, wo_ref, g2_ref, wv_ref, wg_ref, wd_ref, cp_ref,
                o_ref, h_ref, act_ref, uv_ref, ug_ref, *, T, nT, nC):
    i = pl.program_id(1)
    g2 = g2_ref[...]
    x_ext = jnp.concatenate([xp_ref[0], x_ref[0], xn_ref[0]], axis=0)
    m_ext = jnp.concatenate([mp_ref[0], m_ref[0], mn_ref[0]], axis=0)
    x1_ext = x_ext + jnp.dot(m_ext, wo_ref[...], preferred_element_type=f32)
    x = x1_ext[MIX_HALO:MIX_HALO + T]
    hp = jnp.where(i == 0, 0.0, _rms(x1_ext[MIX_HALO - HALO:MIX_HALO], g2))
    hn = jnp.where(i == nT - 1, 0.0, _rms(x1_ext[MIX_HALO + T:MIX_HALO + T + HALO], g2))
    h_ref[...] = jnp.concatenate([hp, _rms(x, g2), hn], axis=0).astype(bf16)

    def conv(u_ref, cp, r0, rb):
        return (cp[rb:rb + 1] + cp[r0:r0 + 1] * u_ref[HALO - 1:HALO - 1 + T]
                + cp[r0 + 1:r0 + 2] * u_ref[HALO:HALO + T] + cp[r0 + 2:r0 + 3] * u_ref[HALO + 1:HALO + 1 + T])

    def up(c, slot):
        h = h_ref[...]
        uv_ref[slot] = jnp.dot(h, wv_ref[c], preferred_element_type=f32)
        ug_ref[slot] = jnp.dot(h, wg_ref[c], preferred_element_type=f32)

    def activate(c, slot):
        cp = cp_ref[c]
        val = conv(uv_ref.at[slot], cp, 0, 6)
        gate = conv(ug_ref.at[slot], cp, 3, 7)
        act_ref[:, c * FF_CHUNK:(c + 1) * FF_CHUNK] = (gate * jax.nn.sigmoid(gate) * val).astype(bf16)

    up(0, 0)
    for c in range(nC):
        if c + 1 < nC:
            up(c + 1, (c + 1) % 2)
        activate(c, c % 2)
    o_ref[0] = x + jnp.dot(act_ref[...], wd_ref[...], preferred_element_type=f32)


def _ffn(x, mix, wo, g2, wv, wg, wd, cp):
    B, S, _ = x.shape
    T = FFN_ROWS
    nT = S // T
    nC = wv.shape[0]
    hb = T // MIX_HALO
    const2 = lambda b, i: (0, 0)
    const3 = lambda b, i: (0, 0, 0)
    single = dict(pipeline_mode=pl.Buffered(1))
    tile = pl.BlockSpec((1, T, D_MODEL), lambda b, i: (b, i, 0))
    prev = pl.BlockSpec((1, MIX_HALO, D_MODEL), lambda b, i: (b, jnp.maximum(i * hb - 1, 0), 0))
    nxt = pl.BlockSpec((1, MIX_HALO, D_MODEL), lambda b, i: (b, jnp.minimum((i + 1) * hb, S // MIX_HALO - 1), 0))
    return pl.pallas_call(
        functools.partial(_ffn_kernel, T=T, nT=nT, nC=nC),
        grid=(B, nT),
        in_specs=[
            tile, prev, nxt,
            tile, prev, nxt,
            pl.BlockSpec(wo.shape, const2, **single),
            pl.BlockSpec((1, D_MODEL), const2),
            pl.BlockSpec(wv.shape, const3, **single),
            pl.BlockSpec(wg.shape, const3, **single),
            pl.BlockSpec(wd.shape, const2, **single),
            pl.BlockSpec(cp.shape, const3),
        ],
        out_specs=pl.BlockSpec((1, T, D_MODEL), lambda b, i: (b, i, 0)),
        out_shape=jax.ShapeDtypeStruct((B, S, D_MODEL), f32),
        scratch_shapes=[pltpu.VMEM((T + 2 * HALO, D_MODEL), bf16), pltpu.VMEM((T, D_FF), bf16),
                        pltpu.VMEM((2, T + 2 * HALO, FF_CHUNK), f32),
                        pltpu.VMEM((2, T + 2 * HALO, FF_CHUNK), f32)],
        compiler_params=pltpu.CompilerParams(
            dimension_semantics=("parallel", "parallel"), vmem_limit_bytes=VMEM_LIMIT),
        name="ffn",
    )(x, x, x, mix, mix, mix, wo, g2, wv, wg, wd, cp)


def _t5_bucket(rp):
    half = T5_BUCKETS // 2
    max_exact = half // 2
    ret = jnp.where(rp > 0, half, 0)
    n = jnp.abs(rp)
    nf = jnp.maximum(n, 1).astype(f32)
    large = max_exact + (jnp.log(nf / max_exact) / math.log(T5_MAX_DIST / max_exact)
                         * (half - max_exact)).astype(jnp.int32)
    large = jnp.minimum(large, half - 1)
    return ret + jnp.where(n < max_exact, n, large)


def _t5_tiles(rel_bias, T):
    assert T >= T5_MAX_DIST
    off = np.arange(T)[None, :] - np.arange(T)[:, None]
    rp = np.stack([off - T, off, off + T]).astype(np.int32)
    bucket = _t5_bucket(jnp.asarray(rp))[None]
    table = jnp.transpose(rel_bias).astype(f32) * LOG2E
    tiles = jnp.zeros((table.shape[0], 3, T, T), f32)
    for b in range(T5_BUCKETS):
        tiles = jnp.where(bucket == b, table[:, b][:, None, None, None], tiles)
    tiles = jnp.concatenate([tiles, jnp.zeros_like(tiles[:, :1])], axis=1)
    far_bucket = _t5_bucket(jnp.asarray([-2 * T, 2 * T], jnp.int32))
    far = jnp.zeros((table.shape[0], 2), f32)
    for b in range(T5_BUCKETS):
        far = jnp.where(far_bucket[None, :] == b, table[:, b][:, None], far)
    return tiles, far


def _na_tiles(na_bias_l):
    kh = NA_ROWS
    c = np.arange(GRID_W)
    cs = np.clip(c - NA_COLS // 2, 0, GRID_W - NA_COLS)
    kc = np.arange(GRID_W)
    valid = (kc[None, :] >= cs[:, None]) & (kc[None, :] < cs[:, None] + NA_COLS)
    dcol = kc[None, :] - c[:, None] + (NA_COLS - 1)
    delta = np.arange(kh)
    ki = np.arange(kh)
    drow = ki[None, :] - delta[:, None] + (NA_ROWS - 1)
    row_sel = (drow[:, :, None] == np.arange(2 * NA_ROWS - 1)).astype(np.float32)
    col_sel = (dcol[None] == np.arange(2 * NA_COLS - 1)[:, None, None]).astype(np.float32)
    hp = lax.Precision.HIGHEST
    tmp = jnp.einsum('hab,bck->hack', na_bias_l.astype(f32), col_sel, precision=hp)
    vals = jnp.einsum('dia,hack->dhcik', row_sel, tmp, precision=hp)
    vals = jnp.where(valid[None, None, :, None, :], vals, NEG)
    return vals.reshape(kh, B_HEADS * GRID_W, kh * GRID_W)


def _in_perm():
    q1, q2, k1, k2 = 0, A_QK, 2 * A_QK, 3 * A_QK
    cols = []
    for a, b in ((q1, q2), (k1, k2)):
        for h in range(A_HEADS):
            cols += list(range(a + h * HEAD_DIM, a + (h + 1) * HEAD_DIM))
            cols += list(range(b + h * HEAD_DIM, b + (h + 1) * HEAD_DIM))
    cols += list(range(4 * A_QK, IN_WIDTH))
    return np.asarray(cols, np.int32)


def _layer_params(l, p):
    scale = HEAD_DIM ** -0.5
    ones = lambda n: jnp.ones((n,), f32)
    tile = lambda g, n: jnp.tile(g.astype(f32), n)
    gain = jnp.concatenate([
        tile(p['qn_a'][l], 2 * A_HEADS) * (scale * LOG2E), tile(p['kn_a'][l], 2 * A_HEADS), ones(A_V),
        tile(p['qn_b'][l], B_HEADS) * scale, tile(p['kn_b'][l], B_HEADS), ones(B_W),
        tile(p['qn_c'][l], C_HEADS) * scale])[None, :]
    nC = D_FF // FF_CHUNK
    w_up = p['w_up'][l].astype(bf16)
    chunks = lambda w: jnp.transpose(w.reshape(D_MODEL, nC, FF_CHUNK), (1, 0, 2))
    cw = p['conv_w'][l].astype(f32)
    cb = p['conv_b'][l].astype(f32)
    cp = jnp.concatenate([cw[:, :D_FF], cw[:, D_FF:], cb[None, :D_FF], cb[None, D_FF:]], axis=0)
    w_out = p['w_out'][l].astype(bf16)
    return dict(
        g1=p['norm1_g'][l].astype(f32)[None, :],
        w_in=p['w_in'][l][:, _in_perm()].astype(bf16),
        gain=gain,
        lamv=jnp.stack([p['lam_q1'][l], p['lam_k1'][l], p['lam_q2'][l], p['lam_k2'][l]]).astype(f32),
        sg=p['subln_g'][l].astype(f32)[None, :],
        na=_na_tiles(p['na_bias'][l]),
        mem_g=p['mem_g'][l].astype(f32)[None, :],
        w_mem=p['w_mem_kv'][l].astype(bf16),
        gain_c=tile(p['kn_c'][l], C_HEADS)[None, :],
        wo=w_out,
        g2=p['norm2_g'][l].astype(f32)[None, :],
        wv=chunks(w_up[:, :D_FF]), wg=chunks(w_up[:, D_FF:]),
        wd=p['w_down'][l].astype(bf16),
        cp=jnp.transpose(cp.reshape(8, nC, FF_CHUNK), (1, 0, 2)),
        lam_init=0.8 - 0.6 * math.exp(-0.3 * l),
    )


def _layer(x, mem, lp, shared):
    qa, ka, va, qb, kb, vb, qc = _proj(x, lp['g1'], lp['w_in'], shared['gmat'], lp['gain'])
    kc, vc = _mem_kv(mem, lp['mem_g'], lp['w_mem'], shared['gmat'], lp['gain_c'])
    oa = _attn_a(qa, ka, va, shared['btiles'], shared['cfar'], lp['lamv'], lp['sg'], lp['lam_init'])
    mix = _local(qb, kb, vb, lp['na'], qc, kc, vc, oa)
    return _ffn(x, mix, lp['wo'], lp['g2'], lp['wv'], lp['wg'], lp['wd'], lp['cp'])


def kernel(x_prompt, x_sample, mem_prompt, mem_sample, norm1_g, w_in, qn_a, kn_a, lam_q1, lam_k1, lam_q2, lam_k2, subln_g, rel_bias, qn_b, kn_b, na_bias, mem_g, w_mem_kv, qn_c, kn_c, w_out, norm2_g, w_up, conv_w, conv_b, w_down):
    p = dict(norm1_g=norm1_g, w_in=w_in, qn_a=qn_a, kn_a=kn_a, lam_q1=lam_q1, lam_k1=lam_k1, lam_q2=lam_q2,
             lam_k2=lam_k2, subln_g=subln_g, qn_b=qn_b, kn_b=kn_b, na_bias=na_bias, mem_g=mem_g,
             w_mem_kv=w_mem_kv, qn_c=qn_c, kn_c=kn_c, w_out=w_out, norm2_g=norm2_g, w_up=w_up,
             conv_w=conv_w, conv_b=conv_b, w_down=w_down)
    depth = w_in.shape[0]
    group = np.arange(LANE_BLOCK) // HEAD_DIM
    gmat = jnp.asarray((group[:, None] == group[None, :]) / HEAD_DIM, bf16)
    btiles, cfar = _t5_tiles(rel_bias, ATT_T)
    shared = dict(gmat=gmat, btiles=btiles, cfar=cfar)
    layers = [_layer_params(l, p) for l in range(depth)]
    y_prompt, y_sample = x_prompt, x_sample
    for lp in layers:
        y_prompt = _layer(y_prompt, mem_prompt, lp, shared)
    for lp in layers:
        y_sample = _layer(y_sample, mem_sample, lp, shared)
    return (y_prompt, y_sample)
```

```python
import functools
import math

import numpy as np
import jax
import jax.numpy as jnp
from jax import lax
from jax.experimental import pallas as pl
from jax.experimental.pallas import tpu as pltpu

D_MODEL = 1024
HEAD_DIM = 64
A_HEADS = 4
A_VDIM = 2 * HEAD_DIM
B_HEADS = 4
C_HEADS = 4
MEM_TOKENS = 256
GRID_W = 64
NA_ROWS = 8
NA_COLS = 16
T5_BUCKETS = 32
T5_MAX_DIST = 128
D_FF = 2816
CONV_WIDTH = 3
EPS = 1e-6

A_QK = A_HEADS * HEAD_DIM
A_V = A_HEADS * A_VDIM
B_W = B_HEADS * HEAD_DIM
C_W = C_HEADS * HEAD_DIM
IN_WIDTH = 4 * A_QK + A_V + 3 * B_W + C_W
MIX_WIDTH = A_V + B_W + C_W

LANE_BLOCK = 256
NEG = -1e30
LOG2E = math.log2(math.e)
SUB = 128
FAR_LANES = 64
VMEM_LIMIT = 56 * 1024 * 1024

PROJ_ROWS = 1024
ATT_T = 512
PIPE_UNROLL = 6
PIPE_TAIL = 4
LOCAL_ROWS = 16
LOCAL_UNROLL = 4
MEM_Q_TILE = 256
FFN_ROWS = 512
FF_CHUNK = 256
HALO = 8
MIX_HALO = 16

f32 = jnp.float32
bf16 = jnp.bfloat16


def _rms(x, g):
    ms = jnp.mean(x * x, axis=-1, keepdims=True)
    return x * lax.rsqrt(ms + EPS) * g


def _group_normed(z, gmat, gain):
    msq = jnp.dot((z * z).astype(bf16), gmat, preferred_element_type=f32)
    return z * lax.rsqrt(msq + EPS) * gain


def _proj_kernel(x_ref, g1_ref, w_ref, gmat_ref, gain_ref,
                 qa_ref, ka_ref, va_ref, qb_ref, kb_ref, vb_ref, qc_ref):
    h = _rms(x_ref[0], g1_ref[...]).astype(bf16)
    gmat = gmat_ref[...]

    def block(n, normed):
        c0 = n * LANE_BLOCK
        z = jnp.dot(h, w_ref[:, c0:c0 + LANE_BLOCK], preferred_element_type=f32)
        if normed:
            z = _group_normed(z, gmat, gain_ref[:, c0:c0 + LANE_BLOCK])
        return z.astype(bf16)

    for pair in range(2):
        for ref, n, normed in ((qa_ref, pair, True), (ka_ref, 2 + pair, True), (va_ref, 4 + pair, False)):
            z = block(n, normed)
            ref[0, 2 * pair] = z[:, :128]
            ref[0, 2 * pair + 1] = z[:, 128:]
    qb_ref[0] = block(6, True)
    kb_ref[0] = block(7, True)
    vb_ref[0] = block(8, False)
    qc_ref[0] = block(9, True)


def _proj(x, g1, w, gmat, gain):
    B, S, _ = x.shape
    tm = PROJ_ROWS
    const = lambda b, i: (0, 0)
    head_spec = pl.BlockSpec((1, A_HEADS, tm, 128), lambda b, i: (b, 0, i, 0))
    tok_spec = pl.BlockSpec((1, tm, LANE_BLOCK), lambda b, i: (b, i, 0))
    head_shape = jax.ShapeDtypeStruct((B, A_HEADS, S, 128), bf16)
    tok_shape = jax.ShapeDtypeStruct((B, S, LANE_BLOCK), bf16)
    return pl.pallas_call(
        _proj_kernel,
        grid=(B, S // tm),
        in_specs=[
            pl.BlockSpec((1, tm, D_MODEL), lambda b, i: (b, i, 0)),
            pl.BlockSpec((1, D_MODEL), const),
            pl.BlockSpec((D_MODEL, IN_WIDTH), const),
            pl.BlockSpec((LANE_BLOCK, LANE_BLOCK), const),
            pl.BlockSpec((1, IN_WIDTH), const),
        ],
        out_specs=[head_spec, head_spec, head_spec, tok_spec, tok_spec, tok_spec, tok_spec],
        out_shape=[head_shape, head_shape, head_shape, tok_shape, tok_shape, tok_shape, tok_shape],
        compiler_params=pltpu.CompilerParams(
            dimension_semantics=("parallel", "parallel"), vmem_limit_bytes=VMEM_LIMIT),
        name="proj",
    )(x, g1, w, gmat, gain)


def _mem_kernel(m_ref, g_ref, w_ref, gmat_ref, gain_ref, kc_ref, vc_ref):
    h = _rms(m_ref[0], g_ref[...]).astype(bf16)
    z = jnp.dot(h, w_ref[...], preferred_element_type=f32)
    kc_ref[0] = _group_normed(z[:, :C_W], gmat_ref[...], gain_ref[...]).astype(bf16)
    vc_ref[0] = z[:, C_W:].astype(bf16)


def _mem_kv(mem, g, w, gmat, gain):
    B, M, _ = mem.shape
    const = lambda b: (0, 0)
    spec = pl.BlockSpec((1, M, C_W), lambda b: (b, 0, 0))
    shape = jax.ShapeDtypeStruct((B, M, C_W), bf16)
    return pl.pallas_call(
        _mem_kernel,
        grid=(B,),
        in_specs=[
            pl.BlockSpec((1, M, D_MODEL), lambda b: (b, 0, 0)),
            pl.BlockSpec((1, D_MODEL), const),
            pl.BlockSpec((D_MODEL, 2 * C_W), const),
            pl.BlockSpec((LANE_BLOCK, LANE_BLOCK), const),
            pl.BlockSpec((1, C_W), const),
        ],
        out_specs=[spec, spec],
        out_shape=[shape, shape],
        compiler_params=pltpu.CompilerParams(dimension_semantics=("parallel",)),
        name="mem_kv",
    )(mem, g, w, gmat, gain)


def _attn_a_kernel(cfar_ref, q_ref, k_ref, v_ref, bt_ref, lam_ref, sg_ref, o_ref,
                   kext_ref, vext_ref, qs_ref, qn_ref, sa_ref, sb_ref, acc_ref, m_ref, *, T, S, lam_init):
    nk = S // T
    nc = T // 128
    h = pl.program_id(1)

    row = lax.broadcasted_iota(jnp.int32, (S, 128), 0)
    lane = lax.broadcasted_iota(jnp.int32, (S, 128), 1)
    kext_ref[:, :128] = k_ref[0, 0]
    onehot = (lane & (FAR_LANES - 1)) == row // SUB
    kext_ref[:, 128:] = jnp.where(onehot, 1.0, 0.0).astype(bf16)
    vext_ref[:, :A_VDIM] = v_ref[0, 0]
    vext_ref[:, A_VDIM:] = jnp.where(lane == 0, 1.0, 0.0).astype(bf16)
    lam = (jnp.exp(jnp.sum(lam_ref[0:1] * lam_ref[1:2], axis=-1, keepdims=True))
           - jnp.exp(jnp.sum(lam_ref[2:3] * lam_ref[3:4], axis=-1, keepdims=True)) + lam_init)

    def stack_queries(i, qn_ref):
        q = q_ref[0, 0, pl.ds(pl.multiple_of(i * T, T), T), :]
        lane = lax.broadcasted_iota(jnp.int32, (T, 128), 1)
        zero = jnp.zeros_like(q)
        qn_ref[:T, :128] = jnp.where(lane < HEAD_DIM, q, zero)
        qn_ref[T:, :128] = jnp.where(lane >= HEAD_DIM, q, zero)
        key_blk = lane & (FAR_LANES - 1)
        qry_blk = i * nc + lax.broadcasted_iota(jnp.int32, (T, 128), 0) // SUB
        c = jnp.where(key_blk <= qry_blk - 2, cfar_ref[h, 0], jnp.where(key_blk >= qry_blk + 2, cfar_ref[h, 1], 0.0))
        c_hi = c.astype(bf16)
        c_lo = (c - c_hi.astype(f32)).astype(bf16)
        ext = jnp.where(lane < FAR_LANES, c_hi, c_lo)
        qn_ref[:T, 128:] = ext
        qn_ref[T:, 128:] = ext

    def reset():
        acc_ref[...] = jnp.zeros_like(acc_ref)
        m_ref[...] = jnp.full_like(m_ref, NEG)

    def finish(i):
        acc = acc_ref[...]
        o1 = acc[:T, :A_VDIM]
        l1 = acc[:T, A_VDIM:A_VDIM + 1]
        o2 = acc[T:, :A_VDIM]
        l2 = acc[T:, A_VDIM:A_VDIM + 1]
        o = o1 / l1 - lam * (o2 / l2)
        o_ref[0, pl.ds(pl.multiple_of(i * T, T), T), :] = (_rms(o, sg_ref[...]) * (1.0 - lam_init)).astype(bf16)

    def scores(j, queries_ref):
        k0 = pl.multiple_of(j * T, T)
        return lax.dot_general(queries_ref[...], kext_ref[pl.ds(k0, T), :],
                               (((1,), (1,)), ((), ())), preferred_element_type=f32)

    def near_tiles(i, slot):
        if slot == 0:
            return {(0, nc - 1): jnp.where(i >= 1, 0, 3)}
        if slot == 2:
            return {(nc - 1, 0): jnp.where(i <= nk - 2, 2, 3)}
        return {(rb, cb): cb - rb + 1 for rb in range(nc) for cb in range(nc) if abs(cb - rb) <= 1}

    def softmax_pv(i, j, s_ref, slot):
        k0 = pl.multiple_of(j * T, T)
        tiles = {} if slot is None else near_tiles(i, slot)
        cols = []
        for cb in range(nc):
            lanes = slice(cb * 128, (cb + 1) * 128)
            if any(key[1] == cb for key in tiles):
                pieces = []
                for half in range(2):
                    for rb in range(nc):
                        r0 = half * T + rb * SUB
                        piece = s_ref[r0:r0 + SUB, lanes]
                        if (rb, cb) in tiles:
                            piece = piece + bt_ref[0, tiles[(rb, cb)]]
                        pieces.append(piece)
                cols.append(jnp.concatenate(pieces, axis=0))
            else:
                cols.append(s_ref[:, lanes])
        mx = cols[0]
        for n in range(1, nc):
            mx = jnp.maximum(mx, cols[n])
        m_prev = m_ref[...]
        m_new = jnp.maximum(m_prev, jnp.broadcast_to(jnp.max(mx, axis=-1, keepdims=True), m_prev.shape))
        alpha = jnp.exp2(m_prev - m_new)
        p = jnp.concatenate([jnp.exp2(col - m_new).astype(bf16) for col in cols], axis=1)
        pv = jnp.dot(p, vext_ref[pl.ds(k0, T), :], preferred_element_type=f32)
        acc_ref[:, :128] = alpha * acc_ref[:, :128] + pv[:, :128]
        acc_ref[:, 128:] = alpha * acc_ref[:, 128:] + pv[:, 128:]
        m_ref[...] = m_new

    def chunk_at(i, t):
        if isinstance(t, int) and t >= nk - 3:
            slot = t - (nk - 3)
            if slot == 0:
                return jnp.where(i >= 1, i - 1, nk - 1)
            if slot == 1:
                return i
            return jnp.where(i <= nk - 2, i + 1, nk - 3)
        lo = jnp.maximum(i - 1, 0)
        n_near = jnp.minimum(i + 2, nk) - lo
        return jnp.where(t < lo, t, t + n_near)

    bufs = (sa_ref, sb_ref)
    n_main = nk - PIPE_TAIL
    n_tiles = S // T

    def query_tile(i, q_cur_ref, q_nxt_ref):
        def block(t0, n, tail):
            for u in range(n):
                if not (tail and u == n - 1):
                    bufs[(u + 1) % 2][...] = scores(chunk_at(i, t0 + u + 1), q_cur_ref)
                elif q_nxt_ref is not None:
                    nxt = jnp.minimum(i + 1, n_tiles - 1)
                    stack_queries(nxt, q_nxt_ref)
                    bufs[(u + 1) % 2][...] = scores(chunk_at(nxt, 0), q_nxt_ref)
                softmax_pv(i, chunk_at(i, t0 + u), bufs[u % 2], u - (n - 3) if tail and u >= n - 3 else None)

        if n_main:
            unroll = max(u for u in range(2, PIPE_UNROLL + 1, 2) if n_main % u == 0)

            def main(g, c):
                block(g * unroll, unroll, False)
                return c

            lax.fori_loop(0, n_main // unroll, main, 0)
        block(n_main, PIPE_TAIL, True)
        finish(i)
        reset()

    reset()
    main_trips = n_main // max(u for u in range(2, PIPE_UNROLL + 1, 2) if n_main % u == 0) if n_main else 0
    if main_trips >= 2:
        def tile_pair(g, carry):
            query_tile(2 * g, qs_ref, qn_ref)
            query_tile(2 * g + 1, qn_ref, qs_ref)
            return carry

        stack_queries(0, qs_ref)
        sa_ref[...] = scores(chunk_at(0, 0), qs_ref)
        lax.fori_loop(0, n_tiles // 2, tile_pair, 0)
    else:
        def single_tile(i, carry):
            stack_queries(i, qs_ref)
            sa_ref[...] = scores(chunk_at(i, 0), qs_ref)
            query_tile(i, qs_ref, None)
            return carry

        lax.fori_loop(0, n_tiles, single_tile, 0)


def _attn_a(qa, ka, va, btiles, cfar, lamv, sg, lam_init):
    B, H, S, _ = qa.shape
    T = ATT_T
    assert S % (2 * T) == 0 and PIPE_TAIL <= S // T and S // SUB <= FAR_LANES and T % SUB == 0
    kernel = functools.partial(_attn_a_kernel, T=T, S=S, lam_init=lam_init)
    return pl.pallas_call(
        kernel,
        grid=(B, H),
        in_specs=[
            pl.BlockSpec(memory_space=pltpu.SMEM),
            pl.BlockSpec((1, 1, S, 128), lambda b, h: (b, h, 0, 0)),
            pl.BlockSpec((1, 1, S, 128), lambda b, h: (b, h, 0, 0)),
            pl.BlockSpec((1, 1, S, 128), lambda b, h: (b, h, 0, 0)),
            pl.BlockSpec((1, 4, SUB, SUB), lambda b, h: (h, 0, 0, 0)),
            pl.BlockSpec((4, HEAD_DIM), lambda b, h: (0, 0)),
            pl.BlockSpec((1, A_VDIM), lambda b, h: (0, 0)),
        ],
        out_specs=pl.BlockSpec((1, S, A_VDIM), lambda b, h: (b, 0, h)),
        out_shape=jax.ShapeDtypeStruct((B, S, A_V), bf16),
        scratch_shapes=[
            pltpu.VMEM((S, 256), bf16),
            pltpu.VMEM((S, 2 * A_VDIM), bf16),
            pltpu.VMEM((2 * T, 256), bf16),
            pltpu.VMEM((2 * T, 256), bf16),
            pltpu.VMEM((2 * T, T), f32),
            pltpu.VMEM((2 * T, T), f32),
            pltpu.VMEM((2 * T, 2 * A_VDIM), f32),
            pltpu.VMEM((2 * T, 128), f32),
        ],
        compiler_params=pltpu.CompilerParams(
            dimension_semantics=("parallel", "parallel"), vmem_limit_bytes=VMEM_LIMIT),
        name="attn_a",
    )(cfar, qa, ka, va, btiles, lamv, sg)


def _head_rows_attention(q, k, v, bias, nh):
    n = q.shape[0]
    rows = nh * n
    row_head = lax.broadcasted_iota(jnp.int32, (rows, LANE_BLOCK), 0) // n
    lane_head = lax.broadcasted_iota(jnp.int32, (rows, LANE_BLOCK), 1) // HEAD_DIM
    own = row_head == lane_head
    qs = jnp.where(own, jnp.concatenate([q] * nh, axis=0), jnp.zeros((rows, LANE_BLOCK), q.dtype))
    s = lax.dot_general(qs, k, (((1,), (1,)), ((), ())), preferred_element_type=f32)
    if bias is not None:
        s = s + bias
    m = jnp.max(s, axis=-1, keepdims=True)
    e = jnp.exp(s - m)
    l = jnp.sum(e, axis=-1, keepdims=True)
    r = jnp.dot(e.astype(bf16), v, preferred_element_type=f32) / l
    r = jnp.where(own, r, 0.0)
    out = r[:n]
    for hh in range(1, nh):
        out = out + r[hh * n:(hh + 1) * n]
    return out


def _local_kernel(qb_ref, kb_ref, vb_ref, bt_ref, qc_ref, kc_ref, vc_ref, o_ref, *, R):
    g = pl.program_id(1)
    kh = NA_ROWS

    def row(rr, c):
        r = g * LOCAL_ROWS + rr
        rs = jnp.clip(r - kh // 2, 0, R - kh)
        q0 = pl.multiple_of(rr * GRID_W, GRID_W)
        k0 = pl.multiple_of(rs * GRID_W, GRID_W)
        q = qb_ref[0, pl.ds(q0, GRID_W), :]
        k = kb_ref[0, pl.ds(k0, kh * GRID_W), :]
        v = vb_ref[0, pl.ds(k0, kh * GRID_W), :]
        o = _head_rows_attention(q, k, v, bt_ref[r - rs], B_HEADS)
        o_ref[0, pl.ds(q0, GRID_W), :B_W] = o.astype(bf16)
        return c

    lax.fori_loop(0, LOCAL_ROWS, row, 0, unroll=LOCAL_UNROLL)

    kc = kc_ref[0]
    vc = vc_ref[0]

    def mem_tile(t, c):
        q0 = pl.multiple_of(t * MEM_Q_TILE, MEM_Q_TILE)
        o = _head_rows_attention(qc_ref[0, pl.ds(q0, MEM_Q_TILE), :], kc, vc, None, C_HEADS)
        o_ref[0, pl.ds(q0, MEM_Q_TILE), B_W:] = o.astype(bf16)
        return c

    lax.fori_loop(0, LOCAL_ROWS * GRID_W // MEM_Q_TILE, mem_tile, 0, unroll=2)


def _local(qb, kb, vb, na_tiles, qc, kc, vc):
    B, S, _ = qb.shape
    R = S // GRID_W
    rows = LOCAL_ROWS * GRID_W
    tile = pl.BlockSpec((1, rows, LANE_BLOCK), lambda b, g: (b, g, 0))
    full = pl.BlockSpec((1, S, LANE_BLOCK), lambda b, g: (b, 0, 0))
    memspec = pl.BlockSpec((1, MEM_TOKENS, LANE_BLOCK), lambda b, g: (b, 0, 0))
    return pl.pallas_call(
        functools.partial(_local_kernel, R=R),
        grid=(B, R // LOCAL_ROWS),
        in_specs=[tile, full, full,
                  pl.BlockSpec(na_tiles.shape, lambda b, g: (0, 0, 0)),
                  tile, memspec, memspec],
        out_specs=pl.BlockSpec((1, rows, B_W + C_W), lambda b, g: (b, g, 0)),
        out_shape=jax.ShapeDtypeStruct((B, S, B_W + C_W), bf16),
        compiler_params=pltpu.CompilerParams(
            dimension_semantics=("parallel", "parallel"), vmem_limit_bytes=VMEM_LIMIT),
        name="local",
    )(qb, kb, vb, na_tiles, qc, kc, vc)


def _ffn_kernel(x_ref, xp_ref, xn_ref, a_ref, ap_ref, an_ref, l_ref, lp_ref, ln_ref, wa_ref, wl_ref, g2_ref,
                wv_ref, wg_ref, wd_ref, cp_ref, o_ref, h_ref, act_ref, uv_ref, ug_ref, *, T, nT, nC):
    i = pl.program_id(1)
    g2 = g2_ref[...]
    x_ext = jnp.concatenate([xp_ref[0], x_ref[0], xn_ref[0]], axis=0)
    a_ext = jnp.concatenate([ap_ref[0], a_ref[0], an_ref[0]], axis=0)
    l_ext = jnp.concatenate([lp_ref[0], l_ref[0], ln_ref[0]], axis=0)
    x1_ext = (x_ext + jnp.dot(a_ext, wa_ref[...], preferred_element_type=f32)
              + jnp.dot(l_ext, wl_ref[...], preferred_element_type=f32))
    x = x1_ext[MIX_HALO:MIX_HALO + T]
    hp = jnp.where(i == 0, 0.0, _rms(x1_ext[MIX_HALO - HALO:MIX_HALO], g2))
    hn = jnp.where(i == nT - 1, 0.0, _rms(x1_ext[MIX_HALO + T:MIX_HALO + T + HALO], g2))
    h_ref[...] = jnp.concatenate([hp, _rms(x, g2), hn], axis=0).astype(bf16)

    def conv(u_ref, cp, r0, rb):
        return (cp[rb:rb + 1] + cp[r0:r0 + 1] * u_ref[HALO - 1:HALO - 1 + T]
                + cp[r0 + 1:r0 + 2] * u_ref[HALO:HALO + T] + cp[r0 + 2:r0 + 3] * u_ref[HALO + 1:HALO + 1 + T])

    def up(c, slot):
        h = h_ref[...]
        uv_ref[slot] = jnp.dot(h, wv_ref[c], preferred_element_type=f32)
        ug_ref[slot] = jnp.dot(h, wg_ref[c], preferred_element_type=f32)

    def activate(c, slot):
        cp = cp_ref[c]
        val = conv(uv_ref.at[slot], cp, 0, 6)
        gate = conv(ug_ref.at[slot], cp, 3, 7)
        act_ref[:, c * FF_CHUNK:(c + 1) * FF_CHUNK] = (gate * jax.nn.sigmoid(gate) * val).astype(bf16)

    up(0, 0)
    for c in range(nC):
        if c + 1 < nC:
            up(c + 1, (c + 1) % 2)
        activate(c, c % 2)
    o_ref[0] = x + jnp.dot(act_ref[...], wd_ref[...], preferred_element_type=f32)


def _ffn(x, oa, ol, wa, wl, g2, wv, wg, wd, cp):
    B, S, _ = x.shape
    T = FFN_ROWS
    nT = S // T
    nC = wv.shape[0]
    hb = T // MIX_HALO
    const2 = lambda b, i: (0, 0)
    const3 = lambda b, i: (0, 0, 0)
    single = dict(pipeline_mode=pl.Buffered(1))
    tile = lambda w: pl.BlockSpec((1, T, w), lambda b, i: (b, i, 0))
    prev = lambda w: pl.BlockSpec((1, MIX_HALO, w), lambda b, i: (b, jnp.maximum(i * hb - 1, 0), 0))
    nxt = lambda w: pl.BlockSpec((1, MIX_HALO, w),
                                 lambda b, i: (b, jnp.minimum((i + 1) * hb, S // MIX_HALO - 1), 0))
    wa_w, wl_w = oa.shape[-1], ol.shape[-1]
    return pl.pallas_call(
        functools.partial(_ffn_kernel, T=T, nT=nT, nC=nC),
        grid=(B, nT),
        in_specs=[
            tile(D_MODEL), prev(D_MODEL), nxt(D_MODEL),
            tile(wa_w), prev(wa_w), nxt(wa_w),
            tile(wl_w), prev(wl_w), nxt(wl_w),
            pl.BlockSpec(wa.shape, const2, **single),
            pl.BlockSpec(wl.shape, const2, **single),
            pl.BlockSpec((1, D_MODEL), const2),
            pl.BlockSpec(wv.shape, const3, **single),
            pl.BlockSpec(wg.shape, const3, **single),
            pl.BlockSpec(wd.shape, const2, **single),
            pl.BlockSpec(cp.shape, const3),
        ],
        out_specs=pl.BlockSpec((1, T, D_MODEL), lambda b, i: (b, i, 0)),
        out_shape=jax.ShapeDtypeStruct((B, S, D_MODEL), f32),
        scratch_shapes=[pltpu.VMEM((T + 2 * HALO, D_MODEL), bf16), pltpu.VMEM((T, D_FF), bf16),
                        pltpu.VMEM((2, T + 2 * HALO, FF_CHUNK), f32),
                        pltpu.VMEM((2, T + 2 * HALO, FF_CHUNK), f32)],
        compiler_params=pltpu.CompilerParams(
            dimension_semantics=("parallel", "parallel"), vmem_limit_bytes=VMEM_LIMIT),
        name="ffn",
    )(x, x, x, oa, oa, oa, ol, ol, ol, wa, wl, g2, wv, wg, wd, cp)


def _t5_bucket(rp):
    half = T5_BUCKETS // 2
    max_exact = half // 2
    ret = jnp.where(rp > 0, half, 0)
    n = jnp.abs(rp)
    nf = jnp.maximum(n, 1).astype(f32)
    large = max_exact + (jnp.log(nf / max_exact) / math.log(T5_MAX_DIST / max_exact)
                         * (half - max_exact)).astype(jnp.int32)
    large = jnp.minimum(large, half - 1)
    return ret + jnp.where(n < max_exact, n, large)


def _t5_tiles(rel_bias, T):
    assert T >= T5_MAX_DIST
    off = np.arange(T)[None, :] - np.arange(T)[:, None]
    rp = np.stack([off - T, off, off + T]).astype(np.int32)
    bucket = _t5_bucket(jnp.asarray(rp))[None]
    table = jnp.transpose(rel_bias).astype(f32) * LOG2E
    tiles = jnp.zeros((table.shape[0], 3, T, T), f32)
    for b in range(T5_BUCKETS):
        tiles = jnp.where(bucket == b, table[:, b][:, None, None, None], tiles)
    tiles = jnp.concatenate([tiles, jnp.zeros_like(tiles[:, :1])], axis=1)
    far_bucket = _t5_bucket(jnp.asarray([-2 * T, 2 * T], jnp.int32))
    far = jnp.zeros((table.shape[0], 2), f32)
    for b in range(T5_BUCKETS):
        far = jnp.where(far_bucket[None, :] == b, table[:, b][:, None], far)
    return tiles, far


def _na_tiles(na_bias_l):
    kh = NA_ROWS
    c = np.arange(GRID_W)
    cs = np.clip(c - NA_COLS // 2, 0, GRID_W - NA_COLS)
    kc = np.arange(GRID_W)
    valid = (kc[None, :] >= cs[:, None]) & (kc[None, :] < cs[:, None] + NA_COLS)
    dcol = kc[None, :] - c[:, None] + (NA_COLS - 1)
    delta = np.arange(kh)
    ki = np.arange(kh)
    drow = ki[None, :] - delta[:, None] + (NA_ROWS - 1)
    row_sel = (drow[:, :, None] == np.arange(2 * NA_ROWS - 1)).astype(np.float32)
    col_sel = (dcol[None] == np.arange(2 * NA_COLS - 1)[:, None, None]).astype(np.float32)
    hp = lax.Precision.HIGHEST
    tmp = jnp.einsum('hab,bck->hack', na_bias_l.astype(f32), col_sel, precision=hp)
    vals = jnp.einsum('dia,hack->dhcik', row_sel, tmp, precision=hp)
    vals = jnp.where(valid[None, None, :, None, :], vals, NEG)
    return vals.reshape(kh, B_HEADS * GRID_W, kh * GRID_W)


def _in_perm():
    q1, q2, k1, k2 = 0, A_QK, 2 * A_QK, 3 * A_QK
    cols = []
    for a, b in ((q1, q2), (k1, k2)):
        for h in range(A_HEADS):
            cols += list(range(a + h * HEAD_DIM, a + (h + 1) * HEAD_DIM))
            cols += list(range(b + h * HEAD_DIM, b + (h + 1) * HEAD_DIM))
    cols += list(range(4 * A_QK, IN_WIDTH))
    return np.asarray(cols, np.int32)


def _layer_params(l, p):
    scale = HEAD_DIM ** -0.5
    ones = lambda n: jnp.ones((n,), f32)
    tile = lambda g, n: jnp.tile(g.astype(f32), n)
    gain = jnp.concatenate([
        tile(p['qn_a'][l], 2 * A_HEADS) * (scale * LOG2E), tile(p['kn_a'][l], 2 * A_HEADS), ones(A_V),
        tile(p['qn_b'][l], B_HEADS) * scale, tile(p['kn_b'][l], B_HEADS), ones(B_W),
        tile(p['qn_c'][l], C_HEADS) * scale])[None, :]
    nC = D_FF // FF_CHUNK
    w_up = p['w_up'][l].astype(bf16)
    chunks = lambda w: jnp.transpose(w.reshape(D_MODEL, nC, FF_CHUNK), (1, 0, 2))
    cw = p['conv_w'][l].astype(f32)
    cb = p['conv_b'][l].astype(f32)
    cp = jnp.concatenate([cw[:, :D_FF], cw[:, D_FF:], cb[None, :D_FF], cb[None, D_FF:]], axis=0)
    w_out = p['w_out'][l].astype(bf16)
    return dict(
        g1=p['norm1_g'][l].astype(f32)[None, :],
        w_in=p['w_in'][l][:, _in_perm()].astype(bf16),
        gain=gain,
        lamv=jnp.stack([p['lam_q1'][l], p['lam_k1'][l], p['lam_q2'][l], p['lam_k2'][l]]).astype(f32),
        sg=p['subln_g'][l].astype(f32)[None, :],
        na=_na_tiles(p['na_bias'][l]),
        mem_g=p['mem_g'][l].astype(f32)[None, :],
        w_mem=p['w_mem_kv'][l].astype(bf16),
        gain_c=tile(p['kn_c'][l], C_HEADS)[None, :],
        wo=w_out,
        g2=p['norm2_g'][l].astype(f32)[None, :],
        wv=chunks(w_up[:, :D_FF]), wg=chunks(w_up[:, D_FF:]),
        wd=p['w_down'][l].astype(bf16),
        cp=jnp.transpose(cp.reshape(8, nC, FF_CHUNK), (1, 0, 2)),
        lam_init=0.8 - 0.6 * math.exp(-0.3 * l),
    )


def _layer(x, mem, lp, shared):
    qa, ka, va, qb, kb, vb, qc = _proj(x, lp['g1'], lp['w_in'], shared['gmat'], lp['gain'])
    kc, vc = _mem_kv(mem, lp['mem_g'], lp['w_mem'], shared['gmat'], lp['gain_c'])
    oa = _attn_a(qa, ka, va, shared['btiles'], shared['cfar'], lp['lamv'], lp['sg'], lp['lam_init'])
    ol = _local(qb, kb, vb, lp['na'], qc, kc, vc)
    return _ffn(x, oa, ol, lp['wo'][:A_V], lp['wo'][A_V:], lp['g2'], lp['wv'], lp['wg'], lp['wd'], lp['cp'])


def kernel(x_prompt, x_sample, mem_prompt, mem_sample, norm1_g, w_in, qn_a, kn_a, lam_q1, lam_k1, lam_q2, lam_k2, subln_g, rel_bias, qn_b, kn_b, na_bias, mem_g, w_mem_kv, qn_c, kn_c, w_out, norm2_g, w_up, conv_w, conv_b, w_down):
    p = dict(norm1_g=norm1_g, w_in=w_in, qn_a=qn_a, kn_a=kn_a, lam_q1=lam_q1, lam_k1=lam_k1, lam_q2=lam_q2,
             lam_k2=lam_k2, subln_g=subln_g, qn_b=qn_b, kn_b=kn_b, na_bias=na_bias, mem_g=mem_g,
             w_mem_kv=w_mem_kv, qn_c=qn_c, kn_c=kn_c, w_out=w_out, norm2_g=norm2_g, w_up=w_up,
             conv_w=conv_w, conv_b=conv_b, w_down=w_down)
    depth = w_in.shape[0]
    group = np.arange(LANE_BLOCK) // HEAD_DIM
    gmat = jnp.asarray((group[:, None] == group[None, :]) / HEAD_DIM, bf16)
    btiles, cfar = _t5_tiles(rel_bias, SUB)
    shared = dict(gmat=gmat, btiles=btiles, cfar=cfar)
    layers = [_layer_params(l, p) for l in range(depth)]
    y_prompt, y_sample = x_prompt, x_sample
    for lp in layers:
        y_prompt = _layer(y_prompt, mem_prompt, lp, shared)
    for lp in layers:
        y_sample = _layer(y_sample, mem_sample, lp, shared)
    return (y_prompt, y_sample)
```

```python
import functools
import math

import numpy as np
import jax
import jax.numpy as jnp
from jax import lax
from jax.experimental import pallas as pl
from jax.experimental.pallas import tpu as pltpu

D_MODEL = 1024
HEAD_DIM = 64
A_HEADS = 4
A_VDIM = 2 * HEAD_DIM
B_HEADS = 4
C_HEADS = 4
MEM_TOKENS = 256
GRID_W = 64
NA_ROWS = 8
NA_COLS = 16
T5_BUCKETS = 32
T5_MAX_DIST = 128
D_FF = 2816
CONV_WIDTH = 3
EPS = 1e-6

A_QK = A_HEADS * HEAD_DIM
A_V = A_HEADS * A_VDIM
B_W = B_HEADS * HEAD_DIM
C_W = C_HEADS * HEAD_DIM
IN_WIDTH = 4 * A_QK + A_V + 3 * B_W + C_W
MIX_WIDTH = A_V + B_W + C_W

LANE_BLOCK = 256
NEG = -1e30
LOG2E = math.log2(math.e)
SUB = 128
FAR_LANES = 64
VMEM_LIMIT = 56 * 1024 * 1024

PROJ_ROWS = 1024
ATT_T = 512
PIPE_UNROLL = 6
PIPE_TAIL = 4
LOCAL_ROWS = 16
LOCAL_UNROLL = 4
MEM_Q_TILE = 256
FFN_ROWS = 512
FF_CHUNK = 256
HALO = 8
MIX_HALO = 16

f32 = jnp.float32
bf16 = jnp.bfloat16


def _rms(x, g):
    ms = jnp.mean(x * x, axis=-1, keepdims=True)
    return x * lax.rsqrt(ms + EPS) * g


def _group_normed(z, gmat, gain):
    msq = jnp.dot((z * z).astype(bf16), gmat, preferred_element_type=f32)
    return z * lax.rsqrt(msq + EPS) * gain


def _proj_kernel(x_ref, g1_ref, w_ref, gmat_ref, gain_ref,
                 qa_ref, ka_ref, va_ref, qb_ref, kb_ref, vb_ref, qc_ref):
    h = _rms(x_ref[0], g1_ref[...]).astype(bf16)
    gmat = gmat_ref[...]

    def block(n, normed):
        c0 = n * LANE_BLOCK
        z = jnp.dot(h, w_ref[:, c0:c0 + LANE_BLOCK], preferred_element_type=f32)
        if normed:
            z = _group_normed(z, gmat, gain_ref[:, c0:c0 + LANE_BLOCK])
        return z.astype(bf16)

    for pair in range(2):
        for ref, n, normed in ((qa_ref, pair, True), (ka_ref, 2 + pair, True), (va_ref, 4 + pair, False)):
            z = block(n, normed)
            ref[0, 2 * pair] = z[:, :128]
            ref[0, 2 * pair + 1] = z[:, 128:]
    qb_ref[0] = block(6, True)
    kb_ref[0] = block(7, True)
    vb_ref[0] = block(8, False)
    qc_ref[0] = block(9, True)


def _proj(x, g1, w, gmat, gain):
    B, S, _ = x.shape
    tm = PROJ_ROWS
    const = lambda b, i: (0, 0)
    head_spec = pl.BlockSpec((1, A_HEADS, tm, 128), lambda b, i: (b, 0, i, 0))
    tok_spec = pl.BlockSpec((1, tm, LANE_BLOCK), lambda b, i: (b, i, 0))
    head_shape = jax.ShapeDtypeStruct((B, A_HEADS, S, 128), bf16)
    tok_shape = jax.ShapeDtypeStruct((B, S, LANE_BLOCK), bf16)
    return pl.pallas_call(
        _proj_kernel,
        grid=(B, S // tm),
        in_specs=[
            pl.BlockSpec((1, tm, D_MODEL), lambda b, i: (b, i, 0)),
            pl.BlockSpec((1, D_MODEL), const),
            pl.BlockSpec((D_MODEL, IN_WIDTH), const),
            pl.BlockSpec((LANE_BLOCK, LANE_BLOCK), const),
            pl.BlockSpec((1, IN_WIDTH), const),
        ],
        out_specs=[head_spec, head_spec, head_spec, tok_spec, tok_spec, tok_spec, tok_spec],
        out_shape=[head_shape, head_shape, head_shape, tok_shape, tok_shape, tok_shape, tok_shape],
        compiler_params=pltpu.CompilerParams(
            dimension_semantics=("parallel", "parallel"), vmem_limit_bytes=VMEM_LIMIT),
        name="proj",
    )(x, g1, w, gmat, gain)


def _mem_kernel(m_ref, g_ref, w_ref, gmat_ref, gain_ref, kc_ref, vc_ref):
    h = _rms(m_ref[0], g_ref[...]).astype(bf16)
    z = jnp.dot(h, w_ref[...], preferred_element_type=f32)
    kc_ref[0] = _group_normed(z[:, :C_W], gmat_ref[...], gain_ref[...]).astype(bf16)
    vc_ref[0] = z[:, C_W:].astype(bf16)


def _mem_kv(mem, g, w, gmat, gain):
    B, M, _ = mem.shape
    const = lambda b: (0, 0)
    spec = pl.BlockSpec((1, M, C_W), lambda b: (b, 0, 0))
    shape = jax.ShapeDtypeStruct((B, M, C_W), bf16)
    return pl.pallas_call(
        _mem_kernel,
        grid=(B,),
        in_specs=[
            pl.BlockSpec((1, M, D_MODEL), lambda b: (b, 0, 0)),
            pl.BlockSpec((1, D_MODEL), const),
            pl.BlockSpec((D_MODEL, 2 * C_W), const),
            pl.BlockSpec((LANE_BLOCK, LANE_BLOCK), const),
            pl.BlockSpec((1, C_W), const),
        ],
        out_specs=[spec, spec],
        out_shape=[shape, shape],
        compiler_params=pltpu.CompilerParams(dimension_semantics=("parallel",)),
        name="mem_kv",
    )(mem, g, w, gmat, gain)


def _attn_a_kernel(cfar_ref, q_ref, k_ref, v_ref, bt_ref, lam_ref, sg_ref, o_ref,
                   kext_ref, vext_ref, qs_ref, qn_ref, sa_ref, sb_ref, acc_ref, m_ref, acc2_ref, m2_ref,
                   *, T, S, lam_init):
    nk = S // T
    nc = T // 128
    h = pl.program_id(1)

    row = lax.broadcasted_iota(jnp.int32, (S, 128), 0)
    lane = lax.broadcasted_iota(jnp.int32, (S, 128), 1)
    kext_ref[:, :128] = k_ref[0, 0]
    onehot = (lane & (FAR_LANES - 1)) == row // SUB
    kext_ref[:, 128:] = jnp.where(onehot, 1.0, 0.0).astype(bf16)
    vext_ref[:, :A_VDIM] = v_ref[0, 0]
    vext_ref[:, A_VDIM:] = jnp.where(lane == 0, 1.0, 0.0).astype(bf16)
    lam = (jnp.exp(jnp.sum(lam_ref[0:1] * lam_ref[1:2], axis=-1, keepdims=True))
           - jnp.exp(jnp.sum(lam_ref[2:3] * lam_ref[3:4], axis=-1, keepdims=True)) + lam_init)

    def stack_queries(i, qn_ref):
        q = q_ref[0, 0, pl.ds(pl.multiple_of(i * T, T), T), :]
        lane = lax.broadcasted_iota(jnp.int32, (T, 128), 1)
        zero = jnp.zeros_like(q)
        qn_ref[:T, :128] = jnp.where(lane < HEAD_DIM, q, zero)
        qn_ref[T:, :128] = jnp.where(lane >= HEAD_DIM, q, zero)
        key_blk = lane & (FAR_LANES - 1)
        qry_blk = i * nc + lax.broadcasted_iota(jnp.int32, (T, 128), 0) // SUB
        c = jnp.where(key_blk <= qry_blk - 2, cfar_ref[h, 0], jnp.where(key_blk >= qry_blk + 2, cfar_ref[h, 1], 0.0))
        c_hi = c.astype(bf16)
        c_lo = (c - c_hi.astype(f32)).astype(bf16)
        ext = jnp.where(lane < FAR_LANES, c_hi, c_lo)
        qn_ref[:T, 128:] = ext
        qn_ref[T:, 128:] = ext

    def reset(state):
        acc_ref, m_ref = state
        acc_ref[...] = jnp.zeros_like(acc_ref)
        m_ref[...] = jnp.full_like(m_ref, NEG)

    def finish(i, state):
        acc = state[0][...]
        o1 = acc[:T, :A_VDIM]
        l1 = acc[:T, A_VDIM:A_VDIM + 1]
        o2 = acc[T:, :A_VDIM]
        l2 = acc[T:, A_VDIM:A_VDIM + 1]
        o = o1 / l1 - lam * (o2 / l2)
        o_ref[0, pl.ds(pl.multiple_of(i * T, T), T), :] = (_rms(o, sg_ref[...]) * (1.0 - lam_init)).astype(bf16)

    def scores(j, queries_ref):
        k0 = pl.multiple_of(j * T, T)
        return lax.dot_general(queries_ref[...], kext_ref[pl.ds(k0, T), :],
                               (((1,), (1,)), ((), ())), preferred_element_type=f32)

    def near_tiles(i, slot):
        if slot == 0:
            return {(0, nc - 1): jnp.where(i >= 1, 0, 3)}
        if slot == 2:
            return {(nc - 1, 0): jnp.where(i <= nk - 2, 2, 3)}
        return {(rb, cb): cb - rb + 1 for rb in range(nc) for cb in range(nc) if abs(cb - rb) <= 1}

    def softmax_pv(i, j, s_ref, slot, state):
        acc_ref, m_ref = state
        k0 = pl.multiple_of(j * T, T)
        tiles = {} if slot is None else near_tiles(i, slot)
        cols = []
        for cb in range(nc):
            lanes = slice(cb * 128, (cb + 1) * 128)
            if any(key[1] == cb for key in tiles):
                pieces = []
                for half in range(2):
                    for rb in range(nc):
                        r0 = half * T + rb * SUB
                        piece = s_ref[r0:r0 + SUB, lanes]
                        if (rb, cb) in tiles:
                            piece = piece + bt_ref[0, tiles[(rb, cb)]]
                        pieces.append(piece)
                cols.append(jnp.concatenate(pieces, axis=0))
            else:
                cols.append(s_ref[:, lanes])
        mx = cols[0]
        for n in range(1, nc):
            mx = jnp.maximum(mx, cols[n])
        m_prev = m_ref[...]
        m_new = jnp.maximum(m_prev, jnp.broadcast_to(jnp.max(mx, axis=-1, keepdims=True), m_prev.shape))
        alpha = jnp.exp2(m_prev - m_new)
        p = jnp.concatenate([jnp.exp2(col - m_new).astype(bf16) for col in cols], axis=1)
        pv = jnp.dot(p, vext_ref[pl.ds(k0, T), :], preferred_element_type=f32)
        acc_ref[:, :128] = alpha * acc_ref[:, :128] + pv[:, :128]
        acc_ref[:, 128:] = alpha * acc_ref[:, 128:] + pv[:, 128:]
        m_ref[...] = m_new

    def chunk_at(i, t):
        if isinstance(t, int) and t >= nk - 3:
            slot = t - (nk - 3)
            if slot == 0:
                return jnp.where(i >= 1, i - 1, nk - 1)
            if slot == 1:
                return i
            return jnp.where(i <= nk - 2, i + 1, nk - 3)
        lo = jnp.maximum(i - 1, 0)
        n_near = jnp.minimum(i + 2, nk) - lo
        return jnp.where(t < lo, t, t + n_near)

    bufs = (sa_ref, sb_ref)
    n_main = nk - PIPE_TAIL
    n_tiles = S // T

    def query_tile(i, q_cur_ref, q_nxt_ref, state, prev=None):
        def block(t0, n, tail):
            if tail and prev is not None:
                finish(*prev)
                reset(prev[1])
            for u in range(n):
                if not (tail and u == n - 1):
                    bufs[(u + 1) % 2][...] = scores(chunk_at(i, t0 + u + 1), q_cur_ref)
                elif q_nxt_ref is not None:
                    nxt = jnp.minimum(i + 1, n_tiles - 1)
                    stack_queries(nxt, q_nxt_ref)
                    bufs[(u + 1) % 2][...] = scores(chunk_at(nxt, 0), q_nxt_ref)
                softmax_pv(i, chunk_at(i, t0 + u), bufs[u % 2], u - (n - 3) if tail and u >= n - 3 else None, state)

        if n_main:
            unroll = max(u for u in range(2, PIPE_UNROLL + 1, 2) if n_main % u == 0)

            def main(g, c):
                block(g * unroll, unroll, False)
                return c

            lax.fori_loop(0, n_main // unroll, main, 0)
        block(n_main, PIPE_TAIL, True)
        if prev is None:
            finish(i, state)
            reset(state)

    even, odd = (acc_ref, m_ref), (acc2_ref, m2_ref)
    reset(even)
    reset(odd)
    main_trips = n_main // max(u for u in range(2, PIPE_UNROLL + 1, 2) if n_main % u == 0) if n_main else 0
    if main_trips >= 2:
        def tile_pair(g, carry):
            query_tile(2 * g, qs_ref, qn_ref, even, prev=(jnp.maximum(2 * g - 1, 0), odd))
            query_tile(2 * g + 1, qn_ref, qs_ref, odd, prev=(2 * g, even))
            return carry

        acc2_ref[:, A_VDIM:] = jnp.ones((2 * T, 128), f32)
        stack_queries(0, qs_ref)
        sa_ref[...] = scores(chunk_at(0, 0), qs_ref)
        lax.fori_loop(0, n_tiles // 2, tile_pair, 0)
        finish(n_tiles - 1, odd)
    else:
        def single_tile(i, carry):
            stack_queries(i, qs_ref)
            sa_ref[...] = scores(chunk_at(i, 0), qs_ref)
            query_tile(i, qs_ref, None, even)
            return carry

        lax.fori_loop(0, n_tiles, single_tile, 0)


def _attn_a(qa, ka, va, btiles, cfar, lamv, sg, lam_init):
    B, H, S, _ = qa.shape
    T = ATT_T
    assert S % (2 * T) == 0 and PIPE_TAIL <= S // T and S // SUB <= FAR_LANES and T % SUB == 0
    kernel = functools.partial(_attn_a_kernel, T=T, S=S, lam_init=lam_init)
    return pl.pallas_call(
        kernel,
        grid=(B, H),
        in_specs=[
            pl.BlockSpec(memory_space=pltpu.SMEM),
            pl.BlockSpec((1, 1, S, 128), lambda b, h: (b, h, 0, 0)),
            pl.BlockSpec((1, 1, S, 128), lambda b, h: (b, h, 0, 0)),
            pl.BlockSpec((1, 1, S, 128), lambda b, h: (b, h, 0, 0)),
            pl.BlockSpec((1, 4, SUB, SUB), lambda b, h: (h, 0, 0, 0)),
            pl.BlockSpec((4, HEAD_DIM), lambda b, h: (0, 0)),
            pl.BlockSpec((1, A_VDIM), lambda b, h: (0, 0)),
        ],
        out_specs=pl.BlockSpec((1, S, A_VDIM), lambda b, h: (b, 0, h)),
        out_shape=jax.ShapeDtypeStruct((B, S, A_V), bf16),
        scratch_shapes=[
            pltpu.VMEM((S, 256), bf16),
            pltpu.VMEM((S, 2 * A_VDIM), bf16),
            pltpu.VMEM((2 * T, 256), bf16),
            pltpu.VMEM((2 * T, 256), bf16),
            pltpu.VMEM((2 * T, T), f32),
            pltpu.VMEM((2 * T, T), f32),
            pltpu.VMEM((2 * T, 2 * A_VDIM), f32),
            pltpu.VMEM((2 * T, 128), f32),
            pltpu.VMEM((2 * T, 2 * A_VDIM), f32),
            pltpu.VMEM((2 * T, 128), f32),
        ],
        compiler_params=pltpu.CompilerParams(
            dimension_semantics=("parallel", "parallel"), vmem_limit_bytes=VMEM_LIMIT),
        name="attn_a",
    )(cfar, qa, ka, va, btiles, lamv, sg)


def _head_rows_attention(q, k, v, bias, nh):
    n = q.shape[0]
    rows = nh * n
    row_head = lax.broadcasted_iota(jnp.int32, (rows, LANE_BLOCK), 0) // n
    lane_head = lax.broadcasted_iota(jnp.int32, (rows, LANE_BLOCK), 1) // HEAD_DIM
    own = row_head == lane_head
    qs = jnp.where(own, jnp.concatenate([q] * nh, axis=0), jnp.zeros((rows, LANE_BLOCK), q.dtype))
    s = lax.dot_general(qs, k, (((1,), (1,)), ((), ())), preferred_element_type=f32)
    if bias is not None:
        s = s + bias
    m = jnp.max(s, axis=-1, keepdims=True)
    e = jnp.exp(s - m)
    l = jnp.sum(e, axis=-1, keepdims=True)
    r = jnp.dot(e.astype(bf16), v, preferred_element_type=f32) / l
    r = jnp.where(own, r, 0.0)
    out = r[:n]
    for hh in range(1, nh):
        out = out + r[hh * n:(hh + 1) * n]
    return out


def _local_kernel(qb_ref, kb_ref, vb_ref, bt_ref, qc_ref, kc_ref, vc_ref, o_ref, *, R):
    g = pl.program_id(1)
    kh = NA_ROWS

    def row(rr, c):
        r = g * LOCAL_ROWS + rr
        rs = jnp.clip(r - kh // 2, 0, R - kh)
        q0 = pl.multiple_of(rr * GRID_W, GRID_W)
        k0 = pl.multiple_of(rs * GRID_W, GRID_W)
        q = qb_ref[0, pl.ds(q0, GRID_W), :]
        k = kb_ref[0, pl.ds(k0, kh * GRID_W), :]
        v = vb_ref[0, pl.ds(k0, kh * GRID_W), :]
        o = _head_rows_attention(q, k, v, bt_ref[r - rs], B_HEADS)
        o_ref[0, pl.ds(q0, GRID_W), :B_W] = o.astype(bf16)
        return c

    lax.fori_loop(0, LOCAL_ROWS, row, 0, unroll=LOCAL_UNROLL)

    kc = kc_ref[0]
    vc = vc_ref[0]

    def mem_tile(t, c):
        q0 = pl.multiple_of(t * MEM_Q_TILE, MEM_Q_TILE)
        o = _head_rows_attention(qc_ref[0, pl.ds(q0, MEM_Q_TILE), :], kc, vc, None, C_HEADS)
        o_ref[0, pl.ds(q0, MEM_Q_TILE), B_W:] = o.astype(bf16)
        return c

    lax.fori_loop(0, LOCAL_ROWS * GRID_W // MEM_Q_TILE, mem_tile, 0, unroll=2)


def _local(qb, kb, vb, na_tiles, qc, kc, vc):
    B, S, _ = qb.shape
    R = S // GRID_W
    rows = LOCAL_ROWS * GRID_W
    tile = pl.BlockSpec((1, rows, LANE_BLOCK), lambda b, g: (b, g, 0))
    full = pl.BlockSpec((1, S, LANE_BLOCK), lambda b, g: (b, 0, 0))
    memspec = pl.BlockSpec((1, MEM_TOKENS, LANE_BLOCK), lambda b, g: (b, 0, 0))
    return pl.pallas_call(
        functools.partial(_local_kernel, R=R),
        grid=(B, R // LOCAL_ROWS),
        in_specs=[tile, full, full,
                  pl.BlockSpec(na_tiles.shape, lambda b, g: (0, 0, 0)),
                  tile, memspec, memspec],
        out_specs=pl.BlockSpec((1, rows, B_W + C_W), lambda b, g: (b, g, 0)),
        out_shape=jax.ShapeDtypeStruct((B, S, B_W + C_W), bf16),
        compiler_params=pltpu.CompilerParams(
            dimension_semantics=("parallel", "parallel"), vmem_limit_bytes=VMEM_LIMIT),
        name="local",
    )(qb, kb, vb, na_tiles, qc, kc, vc)


def _ffn_kernel(x_ref, xp_ref, xn_ref, a_ref, ap_ref, an_ref, l_ref, lp_ref, ln_ref, wa_ref, wl_ref, g2_ref,
                wv_ref, wg_ref, wd_ref, cp_ref, o_ref, h_ref, act_ref, uv_ref, ug_ref, *, T, nT, nC):
    i = pl.program_id(1)
    g2 = g2_ref[...]
    x_ext = jnp.concatenate([xp_ref[0], x_ref[0], xn_ref[0]], axis=0)
    a_ext = jnp.concatenate([ap_ref[0], a_ref[0], an_ref[0]], axis=0)
    l_ext = jnp.concatenate([lp_ref[0], l_ref[0], ln_ref[0]], axis=0)
    x1_ext = (x_ext + jnp.dot(a_ext, wa_ref[...], preferred_element_type=f32)
              + jnp.dot(l_ext, wl_ref[...], preferred_element_type=f32))
    x = x1_ext[MIX_HALO:MIX_HALO + T]
    hp = jnp.where(i == 0, 0.0, _rms(x1_ext[MIX_HALO - HALO:MIX_HALO], g2))
    hn = jnp.where(i == nT - 1, 0.0, _rms(x1_ext[MIX_HALO + T:MIX_HALO + T + HALO], g2))
    h_ref[...] = jnp.concatenate([hp, _rms(x, g2), hn], axis=0).astype(bf16)

    def conv(u_ref, cp, r0, rb):
        return (cp[rb:rb + 1] + cp[r0:r0 + 1] * u_ref[HALO - 1:HALO - 1 + T]
                + cp[r0 + 1:r0 + 2] * u_ref[HALO:HALO + T] + cp[r0 + 2:r0 + 3] * u_ref[HALO + 1:HALO + 1 + T])

    def up(c, slot):
        h = h_ref[...]
        uv_ref[slot] = jnp.dot(h, wv_ref[c], preferred_element_type=f32)
        ug_ref[slot] = jnp.dot(h, wg_ref[c], preferred_element_type=f32)

    def activate(c, slot):
        cp = cp_ref[c]
        val = conv(uv_ref.at[slot], cp, 0, 6)
        gate = conv(ug_ref.at[slot], cp, 3, 7)
        act_ref[:, c * FF_CHUNK:(c + 1) * FF_CHUNK] = (gate * jax.nn.sigmoid(gate) * val).astype(bf16)

    up(0, 0)
    for c in range(nC):
        if c + 1 < nC:
            up(c + 1, (c + 1) % 2)
        activate(c, c % 2)
    o_ref[0] = x + jnp.dot(act_ref[...], wd_ref[...], preferred_element_type=f32)


def _ffn(x, oa, ol, wa, wl, g2, wv, wg, wd, cp):
    B, S, _ = x.shape
    T = FFN_ROWS
    nT = S // T
    nC = wv.shape[0]
    hb = T // MIX_HALO
    const2 = lambda b, i: (0, 0)
    const3 = lambda b, i: (0, 0, 0)
    single = dict(pipeline_mode=pl.Buffered(1))
    tile = lambda w: pl.BlockSpec((1, T, w), lambda b, i: (b, i, 0))
    prev = lambda w: pl.BlockSpec((1, MIX_HALO, w), lambda b, i: (b, jnp.maximum(i * hb - 1, 0), 0))
    nxt = lambda w: pl.BlockSpec((1, MIX_HALO, w),
                                 lambda b, i: (b, jnp.minimum((i + 1) * hb, S // MIX_HALO - 1), 0))
    wa_w, wl_w = oa.shape[-1], ol.shape[-1]
    return pl.pallas_call(
        functools.partial(_ffn_kernel, T=T, nT=nT, nC=nC),
        grid=(B, nT),
        in_specs=[
            tile(D_MODEL), prev(D_MODEL), nxt(D_MODEL),
            tile(wa_w), prev(wa_w), nxt(wa_w),
            tile(wl_w), prev(wl_w), nxt(wl_w),
            pl.BlockSpec(wa.shape, const2, **single),
            pl.BlockSpec(wl.shape, const2, **single),
            pl.BlockSpec((1, D_MODEL), const2),
            pl.BlockSpec(wv.shape, const3, **single),
            pl.BlockSpec(wg.shape, const3, **single),
            pl.BlockSpec(wd.shape, const2, **single),
            pl.BlockSpec(cp.shape, const3),
        ],
        out_specs=pl.BlockSpec((1, T, D_MODEL), lambda b, i: (b, i, 0)),
        out_shape=jax.ShapeDtypeStruct((B, S, D_MODEL), f32),
        scratch_shapes=[pltpu.VMEM((T + 2 * HALO, D_MODEL), bf16), pltpu.VMEM((T, D_FF), bf16),
                        pltpu.VMEM((2, T + 2 * HALO, FF_CHUNK), f32),
                        pltpu.VMEM((2, T + 2 * HALO, FF_CHUNK), f32)],
        compiler_params=pltpu.CompilerParams(
            dimension_semantics=("parallel", "parallel"), vmem_limit_bytes=VMEM_LIMIT),
        name="ffn",
    )(x, x, x, oa, oa, oa, ol, ol, ol, wa, wl, g2, wv, wg, wd, cp)


def _t5_bucket(rp):
    half = T5_BUCKETS // 2
    max_exact = half // 2
    ret = jnp.where(rp > 0, half, 0)
    n = jnp.abs(rp)
    nf = jnp.maximum(n, 1).astype(f32)
    large = max_exact + (jnp.log(nf / max_exact) / math.log(T5_MAX_DIST / max_exact)
                         * (half - max_exact)).astype(jnp.int32)
    large = jnp.minimum(large, half - 1)
    return ret + jnp.where(n < max_exact, n, large)


def _t5_tiles(rel_bias, T):
    assert T >= T5_MAX_DIST
    off = np.arange(T)[None, :] - np.arange(T)[:, None]
    rp = np.stack([off - T, off, off + T]).astype(np.int32)
    bucket = _t5_bucket(jnp.asarray(rp))[None]
    table = jnp.transpose(rel_bias).astype(f32) * LOG2E
    tiles = jnp.zeros((table.shape[0], 3, T, T), f32)
    for b in range(T5_BUCKETS):
        tiles = jnp.where(bucket == b, table[:, b][:, None, None, None], tiles)
    tiles = jnp.concatenate([tiles, jnp.zeros_like(tiles[:, :1])], axis=1)
    far_bucket = _t5_bucket(jnp.asarray([-2 * T, 2 * T], jnp.int32))
    far = jnp.zeros((table.shape[0], 2), f32)
    for b in range(T5_BUCKETS):
        far = jnp.where(far_bucket[None, :] == b, table[:, b][:, None], far)
    return tiles, far


def _na_tiles(na_bias_l):
    kh = NA_ROWS
    c = np.arange(GRID_W)
    cs = np.clip(c - NA_COLS // 2, 0, GRID_W - NA_COLS)
    kc = np.arange(GRID_W)
    valid = (kc[None, :] >= cs[:, None]) & (kc[None, :] < cs[:, None] + NA_COLS)
    dcol = kc[None, :] - c[:, None] + (NA_COLS - 1)
    delta = np.arange(kh)
    ki = np.arange(kh)
    drow = ki[None, :] - delta[:, None] + (NA_ROWS - 1)
    row_sel = (drow[:, :, None] == np.arange(2 * NA_ROWS - 1)).astype(np.float32)
    col_sel = (dcol[None] == np.arange(2 * NA_COLS - 1)[:, None, None]).astype(np.float32)
    hp = lax.Precision.HIGHEST
    tmp = jnp.einsum('hab,bck->hack', na_bias_l.astype(f32), col_sel, precision=hp)
    vals = jnp.einsum('dia,hack->dhcik', row_sel, tmp, precision=hp)
    vals = jnp.where(valid[None, None, :, None, :], vals, NEG)
    return vals.reshape(kh, B_HEADS * GRID_W, kh * GRID_W)


def _in_perm():
    q1, q2, k1, k2 = 0, A_QK, 2 * A_QK, 3 * A_QK
    cols = []
    for a, b in ((q1, q2), (k1, k2)):
        for h in range(A_HEADS):
            cols += list(range(a + h * HEAD_DIM, a + (h + 1) * HEAD_DIM))
            cols += list(range(b + h * HEAD_DIM, b + (h + 1) * HEAD_DIM))
    cols += list(range(4 * A_QK, IN_WIDTH))
    return np.asarray(cols, np.int32)


def _layer_params(l, p):
    scale = HEAD_DIM ** -0.5
    ones = lambda n: jnp.ones((n,), f32)
    tile = lambda g, n: jnp.tile(g.astype(f32), n)
    gain = jnp.concatenate([
        tile(p['qn_a'][l], 2 * A_HEADS) * (scale * LOG2E), tile(p['kn_a'][l], 2 * A_HEADS), ones(A_V),
        tile(p['qn_b'][l], B_HEADS) * scale, tile(p['kn_b'][l], B_HEADS), ones(B_W),
        tile(p['qn_c'][l], C_HEADS) * scale])[None, :]
    nC = D_FF // FF_CHUNK
    w_up = p['w_up'][l].astype(bf16)
    chunks = lambda w: jnp.transpose(w.reshape(D_MODEL, nC, FF_CHUNK), (1, 0, 2))
    cw = p['conv_w'][l].astype(f32)
    cb = p['conv_b'][l].astype(f32)
    cp = jnp.concatenate([cw[:, :D_FF], cw[:, D_FF:], cb[None, :D_FF], cb[None, D_FF:]], axis=0)
    w_out = p['w_out'][l].astype(bf16)
    return dict(
        g1=p['norm1_g'][l].astype(f32)[None, :],
        w_in=p['w_in'][l][:, _in_perm()].astype(bf16),
        gain=gain,
        lamv=jnp.stack([p['lam_q1'][l], p['lam_k1'][l], p['lam_q2'][l], p['lam_k2'][l]]).astype(f32),
        sg=p['subln_g'][l].astype(f32)[None, :],
        na=_na_tiles(p['na_bias'][l]),
        mem_g=p['mem_g'][l].astype(f32)[None, :],
        w_mem=p['w_mem_kv'][l].astype(bf16),
        gain_c=tile(p['kn_c'][l], C_HEADS)[None, :],
        wo=w_out,
        g2=p['norm2_g'][l].astype(f32)[None, :],
        wv=chunks(w_up[:, :D_FF]), wg=chunks(w_up[:, D_FF:]),
        wd=p['w_down'][l].astype(bf16),
        cp=jnp.transpose(cp.reshape(8, nC, FF_CHUNK), (1, 0, 2)),
        lam_init=0.8 - 0.6 * math.exp(-0.3 * l),
    )


def _layer(x, mem, lp, shared):
    qa, ka, va, qb, kb, vb, qc = _proj(x, lp['g1'], lp['w_in'], shared['gmat'], lp['gain'])
    kc, vc = _mem_kv(mem, lp['mem_g'], lp['w_mem'], shared['gmat'], lp['gain_c'])
    oa = _attn_a(qa, ka, va, shared['btiles'], shared['cfar'], lp['lamv'], lp['sg'], lp['lam_init'])
    ol = _local(qb, kb, vb, lp['na'], qc, kc, vc)
    return _ffn(x, oa, ol, lp['wo'][:A_V], lp['wo'][A_V:], lp['g2'], lp['wv'], lp['wg'], lp['wd'], lp['cp'])


def kernel(x_prompt, x_sample, mem_prompt, mem_sample, norm1_g, w_in, qn_a, kn_a, lam_q1, lam_k1, lam_q2, lam_k2, subln_g, rel_bias, qn_b, kn_b, na_bias, mem_g, w_mem_kv, qn_c, kn_c, w_out, norm2_g, w_up, conv_w, conv_b, w_down):
    p = dict(norm1_g=norm1_g, w_in=w_in, qn_a=qn_a, kn_a=kn_a, lam_q1=lam_q1, lam_k1=lam_k1, lam_q2=lam_q2,
             lam_k2=lam_k2, subln_g=subln_g, qn_b=qn_b, kn_b=kn_b, na_bias=na_bias, mem_g=mem_g,
             w_mem_kv=w_mem_kv, qn_c=qn_c, kn_c=kn_c, w_out=w_out, norm2_g=norm2_g, w_up=w_up,
             conv_w=conv_w, conv_b=conv_b, w_down=w_down)
    depth = w_in.shape[0]
    group = np.arange(LANE_BLOCK) // HEAD_DIM
    gmat = jnp.asarray((group[:, None] == group[None, :]) / HEAD_DIM, bf16)
    btiles, cfar = _t5_tiles(rel_bias, SUB)
    shared = dict(gmat=gmat, btiles=btiles, cfar=cfar)
    layers = [_layer_params(l, p) for l in range(depth)]
    y_prompt, y_sample = x_prompt, x_sample
    for lp in layers:
        y_prompt = _layer(y_prompt, mem_prompt, lp, shared)
    for lp in layers:
        y_sample = _layer(y_sample, mem_sample, lp, shared)
    return (y_prompt, y_sample)
```

```python
import functools
import math

import numpy as np
import jax
import jax.numpy as jnp
from jax import lax
from jax.experimental import pallas as pl
from jax.experimental.pallas import tpu as pltpu

D_MODEL = 1024
HEAD_DIM = 64
A_HEADS = 4
A_VDIM = 2 * HEAD_DIM
B_HEADS = 4
C_HEADS = 4
MEM_TOKENS = 256
GRID_W = 64
NA_ROWS = 8
NA_COLS = 16
T5_BUCKETS = 32
T5_MAX_DIST = 128
D_FF = 2816
CONV_WIDTH = 3
EPS = 1e-6

A_QK = A_HEADS * HEAD_DIM
A_V = A_HEADS * A_VDIM
B_W = B_HEADS * HEAD_DIM
C_W = C_HEADS * HEAD_DIM
IN_WIDTH = 4 * A_QK + A_V + 3 * B_W + C_W
MIX_WIDTH = A_V + B_W + C_W

LANE_BLOCK = 256
NEG = -1e30
LOG2E = math.log2(math.e)
SUB = 128
FAR_LANES = 64
VMEM_LIMIT = 56 * 1024 * 1024

PROJ_ROWS = 1024
ATT_T = 512
PIPE_UNROLL = 6
PIPE_TAIL = 4
LOCAL_ROWS = 16
LOCAL_UNROLL = 8
MEM_Q_TILE = 256
FFN_ROWS = 512
FF_CHUNK = 256
HALO = 8
MIX_HALO = 16

f32 = jnp.float32
bf16 = jnp.bfloat16


def _rms(x, g):
    ms = jnp.mean(x * x, axis=-1, keepdims=True)
    return x * lax.rsqrt(ms + EPS) * g


def _group_normed(z, gmat, gain):
    msq = jnp.dot((z * z).astype(bf16), gmat, preferred_element_type=f32)
    return z * lax.rsqrt(msq + EPS) * gain


def _proj_kernel(x_ref, g1_ref, w_ref, gmat_ref, gain_ref,
                 qa_ref, ka_ref, va_ref, qb_ref, kb_ref, vb_ref, qc_ref):
    h = _rms(x_ref[0], g1_ref[...]).astype(bf16)
    gmat = gmat_ref[...]

    def block(n, normed):
        c0 = n * LANE_BLOCK
        z = jnp.dot(h, w_ref[:, c0:c0 + LANE_BLOCK], preferred_element_type=f32)
        if normed:
            z = _group_normed(z, gmat, gain_ref[:, c0:c0 + LANE_BLOCK])
        return z.astype(bf16)

    for pair in range(2):
        for ref, n, normed in ((qa_ref, pair, True), (ka_ref, 2 + pair, True), (va_ref, 4 + pair, False)):
            z = block(n, normed)
            ref[0, 2 * pair] = z[:, :128]
            ref[0, 2 * pair + 1] = z[:, 128:]
    qb_ref[0] = block(6, True)
    kb_ref[0] = block(7, True)
    vb_ref[0] = block(8, False)
    qc_ref[0] = block(9, True)


def _proj(x, g1, w, gmat, gain):
    B, S, _ = x.shape
    tm = PROJ_ROWS
    const = lambda b, i: (0, 0)
    head_spec = pl.BlockSpec((1, A_HEADS, tm, 128), lambda b, i: (b, 0, i, 0))
    tok_spec = pl.BlockSpec((1, tm, LANE_BLOCK), lambda b, i: (b, i, 0))
    head_shape = jax.ShapeDtypeStruct((B, A_HEADS, S, 128), bf16)
    tok_shape = jax.ShapeDtypeStruct((B, S, LANE_BLOCK), bf16)
    return pl.pallas_call(
        _proj_kernel,
        grid=(B, S // tm),
        in_specs=[
            pl.BlockSpec((1, tm, D_MODEL), lambda b, i: (b, i, 0)),
            pl.BlockSpec((1, D_MODEL), const),
            pl.BlockSpec((D_MODEL, IN_WIDTH), const),
            pl.BlockSpec((LANE_BLOCK, LANE_BLOCK), const),
            pl.BlockSpec((1, IN_WIDTH), const),
        ],
        out_specs=[head_spec, head_spec, head_spec, tok_spec, tok_spec, tok_spec, tok_spec],
        out_shape=[head_shape, head_shape, head_shape, tok_shape, tok_shape, tok_shape, tok_shape],
        compiler_params=pltpu.CompilerParams(
            dimension_semantics=("parallel", "parallel"), vmem_limit_bytes=VMEM_LIMIT),
        name="proj",
    )(x, g1, w, gmat, gain)


def _mem_kernel(m_ref, g_ref, w_ref, gmat_ref, gain_ref, kc_ref, vc_ref):
    h = _rms(m_ref[0], g_ref[...]).astype(bf16)
    z = jnp.dot(h, w_ref[...], preferred_element_type=f32)
    kc_ref[0] = _group_normed(z[:, :C_W], gmat_ref[...], gain_ref[...]).astype(bf16)
    vc_ref[0] = z[:, C_W:].astype(bf16)


def _mem_kv(mem, g, w, gmat, gain):
    B, M, _ = mem.shape
    const = lambda b: (0, 0)
    spec = pl.BlockSpec((1, M, C_W), lambda b: (b, 0, 0))
    shape = jax.ShapeDtypeStruct((B, M, C_W), bf16)
    return pl.pallas_call(
        _mem_kernel,
        grid=(B,),
        in_specs=[
            pl.BlockSpec((1, M, D_MODEL), lambda b: (b, 0, 0)),
            pl.BlockSpec((1, D_MODEL), const),
            pl.BlockSpec((D_MODEL, 2 * C_W), const),
            pl.BlockSpec((LANE_BLOCK, LANE_BLOCK), const),
            pl.BlockSpec((1, C_W), const),
        ],
        out_specs=[spec, spec],
        out_shape=[shape, shape],
        compiler_params=pltpu.CompilerParams(dimension_semantics=("parallel",)),
        name="mem_kv",
    )(mem, g, w, gmat, gain)


def _attn_a_kernel(cfar_ref, q_ref, k_ref, v_ref, bt_ref, lam_ref, sg_ref, o_ref,
                   kext_ref, vext_ref, qs_ref, qn_ref, sa_ref, sb_ref, acc_ref, m_ref, acc2_ref, m2_ref,
                   *, T, S, lam_init):
    nk = S // T
    nc = T // 128
    h = pl.program_id(1)

    row = lax.broadcasted_iota(jnp.int32, (S, 128), 0)
    lane = lax.broadcasted_iota(jnp.int32, (S, 128), 1)
    kext_ref[:, :128] = k_ref[0, 0]
    onehot = (lane & (FAR_LANES - 1)) == row // SUB
    kext_ref[:, 128:] = jnp.where(onehot, 1.0, 0.0).astype(bf16)
    vext_ref[:, :A_VDIM] = v_ref[0, 0]
    vext_ref[:, A_VDIM:] = jnp.where(lane == 0, 1.0, 0.0).astype(bf16)
    lam = (jnp.exp(jnp.sum(lam_ref[0:1] * lam_ref[1:2], axis=-1, keepdims=True))
           - jnp.exp(jnp.sum(lam_ref[2:3] * lam_ref[3:4], axis=-1, keepdims=True)) + lam_init)

    def stack_queries(i, qn_ref):
        q = q_ref[0, 0, pl.ds(pl.multiple_of(i * T, T), T), :]
        lane = lax.broadcasted_iota(jnp.int32, (T, 128), 1)
        zero = jnp.zeros_like(q)
        qn_ref[:T, :128] = jnp.where(lane < HEAD_DIM, q, zero)
        qn_ref[T:, :128] = jnp.where(lane >= HEAD_DIM, q, zero)
        key_blk = lane & (FAR_LANES - 1)
        qry_blk = i * nc + lax.broadcasted_iota(jnp.int32, (T, 128), 0) // SUB
        c = jnp.where(key_blk <= qry_blk - 2, cfar_ref[h, 0], jnp.where(key_blk >= qry_blk + 2, cfar_ref[h, 1], 0.0))
        c_hi = c.astype(bf16)
        c_lo = (c - c_hi.astype(f32)).astype(bf16)
        ext = jnp.where(lane < FAR_LANES, c_hi, c_lo)
        qn_ref[:T, 128:] = ext
        qn_ref[T:, 128:] = ext

    def reset(state):
        acc_ref, m_ref = state
        acc_ref[...] = jnp.zeros_like(acc_ref)
        m_ref[...] = jnp.full_like(m_ref, NEG)

    def finish(i, state):
        acc = state[0][...]
        o1 = acc[:T, :A_VDIM]
        l1 = acc[:T, A_VDIM:A_VDIM + 1]
        o2 = acc[T:, :A_VDIM]
        l2 = acc[T:, A_VDIM:A_VDIM + 1]
        o = o1 / l1 - lam * (o2 / l2)
        o_ref[0, pl.ds(pl.multiple_of(i * T, T), T), :] = (_rms(o, sg_ref[...]) * (1.0 - lam_init)).astype(bf16)

    def scores(j, queries_ref):
        k0 = pl.multiple_of(j * T, T)
        return lax.dot_general(queries_ref[...], kext_ref[pl.ds(k0, T), :],
                               (((1,), (1,)), ((), ())), preferred_element_type=f32)

    def near_tiles(i, slot):
        if slot == 0:
            return {(0, nc - 1): jnp.where(i >= 1, 0, 3)}
        if slot == 2:
            return {(nc - 1, 0): jnp.where(i <= nk - 2, 2, 3)}
        return {(rb, cb): cb - rb + 1 for rb in range(nc) for cb in range(nc) if abs(cb - rb) <= 1}

    def softmax_pv(i, j, s_ref, slot, state):
        acc_ref, m_ref = state
        k0 = pl.multiple_of(j * T, T)
        tiles = {} if slot is None else near_tiles(i, slot)
        cols = []
        for cb in range(nc):
            lanes = slice(cb * 128, (cb + 1) * 128)
            if any(key[1] == cb for key in tiles):
                pieces = []
                for half in range(2):
                    for rb in range(nc):
                        r0 = half * T + rb * SUB
                        piece = s_ref[r0:r0 + SUB, lanes]
                        if (rb, cb) in tiles:
                            piece = piece + bt_ref[0, tiles[(rb, cb)]]
                        pieces.append(piece)
                cols.append(jnp.concatenate(pieces, axis=0))
            else:
                cols.append(s_ref[:, lanes])
        mx = cols[0]
        for n in range(1, nc):
            mx = jnp.maximum(mx, cols[n])
        m_prev = m_ref[...]
        m_new = jnp.maximum(m_prev, jnp.broadcast_to(jnp.max(mx, axis=-1, keepdims=True), m_prev.shape))
        alpha = jnp.exp2(m_prev - m_new)
        p = jnp.concatenate([jnp.exp2(col - m_new).astype(bf16) for col in cols], axis=1)
        pv = jnp.dot(p, vext_ref[pl.ds(k0, T), :], preferred_element_type=f32)
        acc_ref[:, :128] = alpha * acc_ref[:, :128] + pv[:, :128]
        acc_ref[:, 128:] = alpha * acc_ref[:, 128:] + pv[:, 128:]
        m_ref[...] = m_new

    def chunk_at(i, t):
        if isinstance(t, int) and t >= nk - 3:
            slot = t - (nk - 3)
            if slot == 0:
                return jnp.where(i >= 1, i - 1, nk - 1)
            if slot == 1:
                return i
            return jnp.where(i <= nk - 2, i + 1, nk - 3)
        lo = jnp.maximum(i - 1, 0)
        n_near = jnp.minimum(i + 2, nk) - lo
        return jnp.where(t < lo, t, t + n_near)

    bufs = (sa_ref, sb_ref)
    n_main = nk - PIPE_TAIL
    n_tiles = S // T

    def query_tile(i, q_cur_ref, q_nxt_ref, state, prev=None):
        def block(t0, n, tail):
            if tail and prev is not None:
                finish(*prev)
                reset(prev[1])
            for u in range(n):
                if not (tail and u == n - 1):
                    bufs[(u + 1) % 2][...] = scores(chunk_at(i, t0 + u + 1), q_cur_ref)
                elif q_nxt_ref is not None:
                    nxt = jnp.minimum(i + 1, n_tiles - 1)
                    stack_queries(nxt, q_nxt_ref)
                    bufs[(u + 1) % 2][...] = scores(chunk_at(nxt, 0), q_nxt_ref)
                softmax_pv(i, chunk_at(i, t0 + u), bufs[u % 2], u - (n - 3) if tail and u >= n - 3 else None, state)

        if n_main:
            unroll = max(u for u in range(2, PIPE_UNROLL + 1, 2) if n_main % u == 0)

            def main(g, c):
                block(g * unroll, unroll, False)
                return c

            lax.fori_loop(0, n_main // unroll, main, 0)
        block(n_main, PIPE_TAIL, True)
        if prev is None:
            finish(i, state)
            reset(state)

    even, odd = (acc_ref, m_ref), (acc2_ref, m2_ref)
    reset(even)
    reset(odd)
    main_trips = n_main // max(u for u in range(2, PIPE_UNROLL + 1, 2) if n_main % u == 0) if n_main else 0
    if main_trips >= 2:
        def tile_pair(g, carry):
            query_tile(2 * g, qs_ref, qn_ref, even, prev=(jnp.maximum(2 * g - 1, 0), odd))
            query_tile(2 * g + 1, qn_ref, qs_ref, odd, prev=(2 * g, even))
            return carry

        acc2_ref[:, A_VDIM:] = jnp.ones((2 * T, 128), f32)
        stack_queries(0, qs_ref)
        sa_ref[...] = scores(chunk_at(0, 0), qs_ref)
        lax.fori_loop(0, n_tiles // 2, tile_pair, 0)
        finish(n_tiles - 1, odd)
    else:
        def single_tile(i, carry):
            stack_queries(i, qs_ref)
            sa_ref[...] = scores(chunk_at(i, 0), qs_ref)
            query_tile(i, qs_ref, None, even)
            return carry

        lax.fori_loop(0, n_tiles, single_tile, 0)


def _attn_a(qa, ka, va, btiles, cfar, lamv, sg, lam_init):
    B, H, S, _ = qa.shape
    T = ATT_T
    assert S % (2 * T) == 0 and PIPE_TAIL <= S // T and S // SUB <= FAR_LANES and T % SUB == 0
    kernel = functools.partial(_attn_a_kernel, T=T, S=S, lam_init=lam_init)
    return pl.pallas_call(
        kernel,
        grid=(B, H),
        in_specs=[
            pl.BlockSpec(memory_space=pltpu.SMEM),
            pl.BlockSpec((1, 1, S, 128), lambda b, h: (b, h, 0, 0)),
            pl.BlockSpec((1, 1, S, 128), lambda b, h: (b, h, 0, 0)),
            pl.BlockSpec((1, 1, S, 128), lambda b, h: (b, h, 0, 0)),
            pl.BlockSpec((1, 4, SUB, SUB), lambda b, h: (h, 0, 0, 0)),
            pl.BlockSpec((4, HEAD_DIM), lambda b, h: (0, 0)),
            pl.BlockSpec((1, A_VDIM), lambda b, h: (0, 0)),
        ],
        out_specs=pl.BlockSpec((1, S, A_VDIM), lambda b, h: (b, 0, h)),
        out_shape=jax.ShapeDtypeStruct((B, S, A_V), bf16),
        scratch_shapes=[
            pltpu.VMEM((S, 256), bf16),
            pltpu.VMEM((S, 2 * A_VDIM), bf16),
            pltpu.VMEM((2 * T, 256), bf16),
            pltpu.VMEM((2 * T, 256), bf16),
            pltpu.VMEM((2 * T, T), f32),
            pltpu.VMEM((2 * T, T), f32),
            pltpu.VMEM((2 * T, 2 * A_VDIM), f32),
            pltpu.VMEM((2 * T, 128), f32),
            pltpu.VMEM((2 * T, 2 * A_VDIM), f32),
            pltpu.VMEM((2 * T, 128), f32),
        ],
        compiler_params=pltpu.CompilerParams(
            dimension_semantics=("parallel", "parallel"), vmem_limit_bytes=VMEM_LIMIT),
        name="attn_a",
    )(cfar, qa, ka, va, btiles, lamv, sg)


def _head_rows_attention(q, k, v, bias, nh):
    n = q.shape[0]
    rows = nh * n
    row_head = lax.broadcasted_iota(jnp.int32, (rows, LANE_BLOCK), 0) // n
    lane_head = lax.broadcasted_iota(jnp.int32, (rows, LANE_BLOCK), 1) // HEAD_DIM
    own = row_head == lane_head
    qs = jnp.where(own, jnp.concatenate([q] * nh, axis=0), jnp.zeros((rows, LANE_BLOCK), q.dtype))
    s = lax.dot_general(qs, k, (((1,), (1,)), ((), ())), preferred_element_type=f32)
    if bias is not None:
        s = s + bias
    m = jnp.max(s, axis=-1, keepdims=True)
    e = jnp.exp(s - m)
    l = jnp.sum(e, axis=-1, keepdims=True)
    r = jnp.dot(e.astype(bf16), v, preferred_element_type=f32) / l
    r = jnp.where(own, r, 0.0)
    out = r[:n]
    for hh in range(1, nh):
        out = out + r[hh * n:(hh + 1) * n]
    return out


def _local_kernel(qb_ref, kb_ref, vb_ref, bt_ref, qc_ref, kc_ref, vc_ref, o_ref, *, R):
    g = pl.program_id(1)
    kh = NA_ROWS

    def row(rr, c):
        r = g * LOCAL_ROWS + rr
        rs = jnp.clip(r - kh // 2, 0, R - kh)
        q0 = pl.multiple_of(rr * GRID_W, GRID_W)
        k0 = pl.multiple_of(rs * GRID_W, GRID_W)
        q = qb_ref[0, pl.ds(q0, GRID_W), :]
        k = kb_ref[0, pl.ds(k0, kh * GRID_W), :]
        v = vb_ref[0, pl.ds(k0, kh * GRID_W), :]
        o = _head_rows_attention(q, k, v, bt_ref[r - rs], B_HEADS)
        o_ref[0, pl.ds(q0, GRID_W), :B_W] = o.astype(bf16)
        return c

    lax.fori_loop(0, LOCAL_ROWS, row, 0, unroll=LOCAL_UNROLL)

    kc = kc_ref[0]
    vc = vc_ref[0]

    def mem_tile(t, c):
        q0 = pl.multiple_of(t * MEM_Q_TILE, MEM_Q_TILE)
        o = _head_rows_attention(qc_ref[0, pl.ds(q0, MEM_Q_TILE), :], kc, vc, None, C_HEADS)
        o_ref[0, pl.ds(q0, MEM_Q_TILE), B_W:] = o.astype(bf16)
        return c

    lax.fori_loop(0, LOCAL_ROWS * GRID_W // MEM_Q_TILE, mem_tile, 0, unroll=True)


def _local(qb, kb, vb, na_tiles, qc, kc, vc):
    B, S, _ = qb.shape
    R = S // GRID_W
    rows = LOCAL_ROWS * GRID_W
    tile = pl.BlockSpec((1, rows, LANE_BLOCK), lambda b, g: (b, g, 0))
    full = pl.BlockSpec((1, S, LANE_BLOCK), lambda b, g: (b, 0, 0))
    memspec = pl.BlockSpec((1, MEM_TOKENS, LANE_BLOCK), lambda b, g: (b, 0, 0))
    return pl.pallas_call(
        functools.partial(_local_kernel, R=R),
        grid=(B, R // LOCAL_ROWS),
        in_specs=[tile, full, full,
                  pl.BlockSpec(na_tiles.shape, lambda b, g: (0, 0, 0)),
                  tile, memspec, memspec],
        out_specs=pl.BlockSpec((1, rows, B_W + C_W), lambda b, g: (b, g, 0)),
        out_shape=jax.ShapeDtypeStruct((B, S, B_W + C_W), bf16),
        compiler_params=pltpu.CompilerParams(
            dimension_semantics=("parallel", "parallel"), vmem_limit_bytes=VMEM_LIMIT),
        name="local",
    )(qb, kb, vb, na_tiles, qc, kc, vc)


def _ffn_kernel(x_ref, xp_ref, xn_ref, a_ref, ap_ref, an_ref, l_ref, lp_ref, ln_ref, wa_ref, wl_ref, g2_ref,
                wv_ref, wg_ref, wd_ref, cp_ref, o_ref, h_ref, act_ref, uv_ref, ug_ref, *, T, nT, nC):
    i = pl.program_id(1)
    g2 = g2_ref[...]
    x_ext = jnp.concatenate([xp_ref[0], x_ref[0], xn_ref[0]], axis=0)
    a_ext = jnp.concatenate([ap_ref[0], a_ref[0], an_ref[0]], axis=0)
    l_ext = jnp.concatenate([lp_ref[0], l_ref[0], ln_ref[0]], axis=0)
    x1_ext = (x_ext + jnp.dot(a_ext, wa_ref[...], preferred_element_type=f32)
              + jnp.dot(l_ext, wl_ref[...], preferred_element_type=f32))
    x = x1_ext[MIX_HALO:MIX_HALO + T]
    hp = jnp.where(i == 0, 0.0, _rms(x1_ext[MIX_HALO - HALO:MIX_HALO], g2))
    hn = jnp.where(i == nT - 1, 0.0, _rms(x1_ext[MIX_HALO + T:MIX_HALO + T + HALO], g2))
    h_ref[...] = jnp.concatenate([hp, _rms(x, g2), hn], axis=0).astype(bf16)

    def conv(u_ref, cp, r0, rb):
        return (cp[rb:rb + 1] + cp[r0:r0 + 1] * u_ref[HALO - 1:HALO - 1 + T]
                + cp[r0 + 1:r0 + 2] * u_ref[HALO:HALO + T] + cp[r0 + 2:r0 + 3] * u_ref[HALO + 1:HALO + 1 + T])

    def up(c, slot):
        h = h_ref[...]
        uv_ref[slot] = jnp.dot(h, wv_ref[c], preferred_element_type=f32)
        ug_ref[slot] = jnp.dot(h, wg_ref[c], preferred_element_type=f32)

    def activate(c, slot):
        cp = cp_ref[c]
        val = conv(uv_ref.at[slot], cp, 0, 6)
        gate = conv(ug_ref.at[slot], cp, 3, 7)
        act_ref[:, c * FF_CHUNK:(c + 1) * FF_CHUNK] = (gate * jax.nn.sigmoid(gate) * val).astype(bf16)

    up(0, 0)
    for c in range(nC):
        if c + 1 < nC:
            up(c + 1, (c + 1) % 2)
        activate(c, c % 2)
    o_ref[0] = x + jnp.dot(act_ref[...], wd_ref[...], preferred_element_type=f32)


def _ffn(x, oa, ol, wa, wl, g2, wv, wg, wd, cp):
    B, S, _ = x.shape
    T = FFN_ROWS
    nT = S // T
    nC = wv.shape[0]
    hb = T // MIX_HALO
    const2 = lambda b, i: (0, 0)
    const3 = lambda b, i: (0, 0, 0)
    single = dict(pipeline_mode=pl.Buffered(1))
    tile = lambda w: pl.BlockSpec((1, T, w), lambda b, i: (b, i, 0))
    prev = lambda w: pl.BlockSpec((1, MIX_HALO, w), lambda b, i: (b, jnp.maximum(i * hb - 1, 0), 0))
    nxt = lambda w: pl.BlockSpec((1, MIX_HALO, w),
                                 lambda b, i: (b, jnp.minimum((i + 1) * hb, S // MIX_HALO - 1), 0))
    wa_w, wl_w = oa.shape[-1], ol.shape[-1]
    return pl.pallas_call(
        functools.partial(_ffn_kernel, T=T, nT=nT, nC=nC),
        grid=(B, nT),
        in_specs=[
            tile(D_MODEL), prev(D_MODEL), nxt(D_MODEL),
            tile(wa_w), prev(wa_w), nxt(wa_w),
            tile(wl_w), prev(wl_w), nxt(wl_w),
            pl.BlockSpec(wa.shape, const2, **single),
            pl.BlockSpec(wl.shape, const2, **single),
            pl.BlockSpec((1, D_MODEL), const2),
            pl.BlockSpec(wv.shape, const3, **single),
            pl.BlockSpec(wg.shape, const3, **single),
            pl.BlockSpec(wd.shape, const2, **single),
            pl.BlockSpec(cp.shape, const3),
        ],
        out_specs=pl.BlockSpec((1, T, D_MODEL), lambda b, i: (b, i, 0)),
        out_shape=jax.ShapeDtypeStruct((B, S, D_MODEL), f32),
        scratch_shapes=[pltpu.VMEM((T + 2 * HALO, D_MODEL), bf16), pltpu.VMEM((T, D_FF), bf16),
                        pltpu.VMEM((2, T + 2 * HALO, FF_CHUNK), f32),
                        pltpu.VMEM((2, T + 2 * HALO, FF_CHUNK), f32)],
        compiler_params=pltpu.CompilerParams(
            dimension_semantics=("parallel", "parallel"), vmem_limit_bytes=VMEM_LIMIT),
        name="ffn",
    )(x, x, x, oa, oa, oa, ol, ol, ol, wa, wl, g2, wv, wg, wd, cp)


def _t5_bucket(rp):
    half = T5_BUCKETS // 2
    max_exact = half // 2
    ret = jnp.where(rp > 0, half, 0)
    n = jnp.abs(rp)
    nf = jnp.maximum(n, 1).astype(f32)
    large = max_exact + (jnp.log(nf / max_exact) / math.log(T5_MAX_DIST / max_exact)
                         * (half - max_exact)).astype(jnp.int32)
    large = jnp.minimum(large, half - 1)
    return ret + jnp.where(n < max_exact, n, large)


def _t5_tiles(rel_bias, T):
    assert T >= T5_MAX_DIST
    off = np.arange(T)[None, :] - np.arange(T)[:, None]
    rp = np.stack([off - T, off, off + T]).astype(np.int32)
    bucket = _t5_bucket(jnp.asarray(rp))[None]
    table = jnp.transpose(rel_bias).astype(f32) * LOG2E
    tiles = jnp.zeros((table.shape[0], 3, T, T), f32)
    for b in range(T5_BUCKETS):
        tiles = jnp.where(bucket == b, table[:, b][:, None, None, None], tiles)
    tiles = jnp.concatenate([tiles, jnp.zeros_like(tiles[:, :1])], axis=1)
    far_bucket = _t5_bucket(jnp.asarray([-2 * T, 2 * T], jnp.int32))
    far = jnp.zeros((table.shape[0], 2), f32)
    for b in range(T5_BUCKETS):
        far = jnp.where(far_bucket[None, :] == b, table[:, b][:, None], far)
    return tiles, far


def _na_tiles(na_bias_l):
    kh = NA_ROWS
    c = np.arange(GRID_W)
    cs = np.clip(c - NA_COLS // 2, 0, GRID_W - NA_COLS)
    kc = np.arange(GRID_W)
    valid = (kc[None, :] >= cs[:, None]) & (kc[None, :] < cs[:, None] + NA_COLS)
    dcol = kc[None, :] - c[:, None] + (NA_COLS - 1)
    delta = np.arange(kh)
    ki = np.arange(kh)
    drow = ki[None, :] - delta[:, None] + (NA_ROWS - 1)
    row_sel = (drow[:, :, None] == np.arange(2 * NA_ROWS - 1)).astype(np.float32)
    col_sel = (dcol[None] == np.arange(2 * NA_COLS - 1)[:, None, None]).astype(np.float32)
    hp = lax.Precision.HIGHEST
    tmp = jnp.einsum('hab,bck->hack', na_bias_l.astype(f32), col_sel, precision=hp)
    vals = jnp.einsum('dia,hack->dhcik', row_sel, tmp, precision=hp)
    vals = jnp.where(valid[None, None, :, None, :], vals, NEG)
    return vals.reshape(kh, B_HEADS * GRID_W, kh * GRID_W)


def _in_perm():
    q1, q2, k1, k2 = 0, A_QK, 2 * A_QK, 3 * A_QK
    cols = []
    for a, b in ((q1, q2), (k1, k2)):
        for h in range(A_HEADS):
            cols += list(range(a + h * HEAD_DIM, a + (h + 1) * HEAD_DIM))
            cols += list(range(b + h * HEAD_DIM, b + (h + 1) * HEAD_DIM))
    cols += list(range(4 * A_QK, IN_WIDTH))
    return np.asarray(cols, np.int32)


def _layer_params(l, p):
    scale = HEAD_DIM ** -0.5
    ones = lambda n: jnp.ones((n,), f32)
    tile = lambda g, n: jnp.tile(g.astype(f32), n)
    gain = jnp.concatenate([
        tile(p['qn_a'][l], 2 * A_HEADS) * (scale * LOG2E), tile(p['kn_a'][l], 2 * A_HEADS), ones(A_V),
        tile(p['qn_b'][l], B_HEADS) * scale, tile(p['kn_b'][l], B_HEADS), ones(B_W),
        tile(p['qn_c'][l], C_HEADS) * scale])[None, :]
    nC = D_FF // FF_CHUNK
    w_up = p['w_up'][l].astype(bf16)
    chunks = lambda w: jnp.transpose(w.reshape(D_MODEL, nC, FF_CHUNK), (1, 0, 2))
    cw = p['conv_w'][l].astype(f32)
    cb = p['conv_b'][l].astype(f32)
    cp = jnp.concatenate([cw[:, :D_FF], cw[:, D_FF:], cb[None, :D_FF], cb[None, D_FF:]], axis=0)
    w_out = p['w_out'][l].astype(bf16)
    return dict(
        g1=p['norm1_g'][l].astype(f32)[None, :],
        w_in=p['w_in'][l][:, _in_perm()].astype(bf16),
        gain=gain,
        lamv=jnp.stack([p['lam_q1'][l], p['lam_k1'][l], p['lam_q2'][l], p['lam_k2'][l]]).astype(f32),
        sg=p['subln_g'][l].astype(f32)[None, :],
        na=_na_tiles(p['na_bias'][l]),
        mem_g=p['mem_g'][l].astype(f32)[None, :],
        w_mem=p['w_mem_kv'][l].astype(bf16),
        gain_c=tile(p['kn_c'][l], C_HEADS)[None, :],
        wo=w_out,
        g2=p['norm2_g'][l].astype(f32)[None, :],
        wv=chunks(w_up[:, :D_FF]), wg=chunks(w_up[:, D_FF:]),
        wd=p['w_down'][l].astype(bf16),
        cp=jnp.transpose(cp.reshape(8, nC, FF_CHUNK), (1, 0, 2)),
        lam_init=0.8 - 0.6 * math.exp(-0.3 * l),
    )


def _layer(x, mem, lp, shared):
    qa, ka, va, qb, kb, vb, qc = _proj(x, lp['g1'], lp['w_in'], shared['gmat'], lp['gain'])
    kc, vc = _mem_kv(mem, lp['mem_g'], lp['w_mem'], shared['gmat'], lp['gain_c'])
    oa = _attn_a(qa, ka, va, shared['btiles'], shared['cfar'], lp['lamv'], lp['sg'], lp['lam_init'])
    ol = _local(qb, kb, vb, lp['na'], qc, kc, vc)
    return _ffn(x, oa, ol, lp['wo'][:A_V], lp['wo'][A_V:], lp['g2'], lp['wv'], lp['wg'], lp['wd'], lp['cp'])


def kernel(x_prompt, x_sample, mem_prompt, mem_sample, norm1_g, w_in, qn_a, kn_a, lam_q1, lam_k1, lam_q2, lam_k2, subln_g, rel_bias, qn_b, kn_b, na_bias, mem_g, w_mem_kv, qn_c, kn_c, w_out, norm2_g, w_up, conv_w, conv_b, w_down):
    p = dict(norm1_g=norm1_g, w_in=w_in, qn_a=qn_a, kn_a=kn_a, lam_q1=lam_q1, lam_k1=lam_k1, lam_q2=lam_q2,
             lam_k2=lam_k2, subln_g=subln_g, qn_b=qn_b, kn_b=kn_b, na_bias=na_bias, mem_g=mem_g,
             w_mem_kv=w_mem_kv, qn_c=qn_c, kn_c=kn_c, w_out=w_out, norm2_g=norm2_g, w_up=w_up,
             conv_w=conv_w, conv_b=conv_b, w_down=w_down)
    depth = w_in.shape[0]
    group = np.arange(LANE_BLOCK) // HEAD_DIM
    gmat = jnp.asarray((group[:, None] == group[None, :]) / HEAD_DIM, bf16)
    btiles, cfar = _t5_tiles(rel_bias, SUB)
    shared = dict(gmat=gmat, btiles=btiles, cfar=cfar)
    layers = [_layer_params(l, p) for l in range(depth)]
    y_prompt, y_sample = x_prompt, x_sample
    for lp in layers:
        y_prompt = _layer(y_prompt, mem_prompt, lp, shared)
    for lp in layers:
        y_sample = _layer(y_sample, mem_sample, lp, shared)
    return (y_prompt, y_sample)
```

```python
import functools
import math

import numpy as np
import jax
import jax.numpy as jnp
from jax import lax
from jax.experimental import pallas as pl
from jax.experimental.pallas import tpu as pltpu

D_MODEL = 1024
HEAD_DIM = 64
A_HEADS = 4
A_VDIM = 2 * HEAD_DIM
B_HEADS = 4
C_HEADS = 4
MEM_TOKENS = 256
GRID_W = 64
NA_ROWS = 8
NA_COLS = 16
T5_BUCKETS = 32
T5_MAX_DIST = 128
D_FF = 2816
CONV_WIDTH = 3
EPS = 1e-6

A_QK = A_HEADS * HEAD_DIM
A_V = A_HEADS * A_VDIM
B_W = B_HEADS * HEAD_DIM
C_W = C_HEADS * HEAD_DIM
IN_WIDTH = 4 * A_QK + A_V + 3 * B_W + C_W
MIX_WIDTH = A_V + B_W + C_W

LANE_BLOCK = 256
NEG = -1e30
LOG2E = math.log2(math.e)
SUB = 128
FAR_LANES = 64
VMEM_LIMIT = 56 * 1024 * 1024

PROJ_ROWS = 2048
ATT_T = 512
PIPE_UNROLL = 6
PIPE_TAIL = 4
LOCAL_ROWS = 16
LOCAL_UNROLL = 8
MEM_Q_TILE = 256
FFN_ROWS = 512
FF_CHUNK = 256
HALO = 8
MIX_HALO = 16

f32 = jnp.float32
bf16 = jnp.bfloat16


def _rms(x, g):
    ms = jnp.mean(x * x, axis=-1, keepdims=True)
    return x * lax.rsqrt(ms + EPS) * g


def _group_normed(z, gmat, gain):
    msq = jnp.dot((z * z).astype(bf16), gmat, preferred_element_type=f32)
    return z * lax.rsqrt(msq + EPS) * gain


def _proj_kernel(x_ref, g1_ref, w_ref, gmat_ref, gain_ref,
                 qa_ref, ka_ref, va_ref, qb_ref, kb_ref, vb_ref, qc_ref):
    h = _rms(x_ref[0], g1_ref[...]).astype(bf16)
    gmat = gmat_ref[...]

    def block(n, normed):
        c0 = n * LANE_BLOCK
        z = jnp.dot(h, w_ref[:, c0:c0 + LANE_BLOCK], preferred_element_type=f32)
        if normed:
            z = _group_normed(z, gmat, gain_ref[:, c0:c0 + LANE_BLOCK])
        return z.astype(bf16)

    for pair in range(2):
        for ref, n, normed in ((qa_ref, pair, True), (ka_ref, 2 + pair, True), (va_ref, 4 + pair, False)):
            z = block(n, normed)
            ref[0, 2 * pair] = z[:, :128]
            ref[0, 2 * pair + 1] = z[:, 128:]
    qb_ref[0] = block(6, True)
    kb_ref[0] = block(7, True)
    vb_ref[0] = block(8, False)
    qc_ref[0] = block(9, True)


def _proj(x, g1, w, gmat, gain):
    B, S, _ = x.shape
    tm = PROJ_ROWS
    const = lambda b, i: (0, 0)
    head_spec = pl.BlockSpec((1, A_HEADS, tm, 128), lambda b, i: (b, 0, i, 0))
    tok_spec = pl.BlockSpec((1, tm, LANE_BLOCK), lambda b, i: (b, i, 0))
    head_shape = jax.ShapeDtypeStruct((B, A_HEADS, S, 128), bf16)
    tok_shape = jax.ShapeDtypeStruct((B, S, LANE_BLOCK), bf16)
    return pl.pallas_call(
        _proj_kernel,
        grid=(B, S // tm),
        in_specs=[
            pl.BlockSpec((1, tm, D_MODEL), lambda b, i: (b, i, 0)),
            pl.BlockSpec((1, D_MODEL), const),
            pl.BlockSpec((D_MODEL, IN_WIDTH), const),
            pl.BlockSpec((LANE_BLOCK, LANE_BLOCK), const),
            pl.BlockSpec((1, IN_WIDTH), const),
        ],
        out_specs=[head_spec, head_spec, head_spec, tok_spec, tok_spec, tok_spec, tok_spec],
        out_shape=[head_shape, head_shape, head_shape, tok_shape, tok_shape, tok_shape, tok_shape],
        compiler_params=pltpu.CompilerParams(
            dimension_semantics=("parallel", "parallel"), vmem_limit_bytes=VMEM_LIMIT),
        name="proj",
    )(x, g1, w, gmat, gain)


def _mem_kernel(m_ref, g_ref, w_ref, gmat_ref, gain_ref, kc_ref, vc_ref):
    h = _rms(m_ref[0], g_ref[...]).astype(bf16)
    z = jnp.dot(h, w_ref[...], preferred_element_type=f32)
    kc_ref[0] = _group_normed(z[:, :C_W], gmat_ref[...], gain_ref[...]).astype(bf16)
    vc_ref[0] = z[:, C_W:].astype(bf16)


def _mem_kv(mem, g, w, gmat, gain):
    B, M, _ = mem.shape
    const = lambda b: (0, 0)
    spec = pl.BlockSpec((1, M, C_W), lambda b: (b, 0, 0))
    shape = jax.ShapeDtypeStruct((B, M, C_W), bf16)
    return pl.pallas_call(
        _mem_kernel,
        grid=(B,),
        in_specs=[
            pl.BlockSpec((1, M, D_MODEL), lambda b: (b, 0, 0)),
            pl.BlockSpec((1, D_MODEL), const),
            pl.BlockSpec((D_MODEL, 2 * C_W), const),
            pl.BlockSpec((LANE_BLOCK, LANE_BLOCK), const),
            pl.BlockSpec((1, C_W), const),
        ],
        out_specs=[spec, spec],
        out_shape=[shape, shape],
        compiler_params=pltpu.CompilerParams(dimension_semantics=("parallel",)),
        name="mem_kv",
    )(mem, g, w, gmat, gain)


def _attn_a_kernel(cfar_ref, q_ref, k_ref, v_ref, bt_ref, lam_ref, sg_ref, o_ref,
                   kext_ref, vext_ref, qs_ref, qn_ref, sa_ref, sb_ref, acc_ref, m_ref, acc2_ref, m2_ref,
                   *, T, S, lam_init):
    nk = S // T
    nc = T // 128
    h = pl.program_id(1)

    row = lax.broadcasted_iota(jnp.int32, (S, 128), 0)
    lane = lax.broadcasted_iota(jnp.int32, (S, 128), 1)
    kext_ref[:, :128] = k_ref[0, 0]
    onehot = (lane & (FAR_LANES - 1)) == row // SUB
    kext_ref[:, 128:] = jnp.where(onehot, 1.0, 0.0).astype(bf16)
    vext_ref[:, :A_VDIM] = v_ref[0, 0]
    vext_ref[:, A_VDIM:] = jnp.where(lane == 0, 1.0, 0.0).astype(bf16)
    lam = (jnp.exp(jnp.sum(lam_ref[0:1] * lam_ref[1:2], axis=-1, keepdims=True))
           - jnp.exp(jnp.sum(lam_ref[2:3] * lam_ref[3:4], axis=-1, keepdims=True)) + lam_init)

    def stack_queries(i, qn_ref):
        q = q_ref[0, 0, pl.ds(pl.multiple_of(i * T, T), T), :]
        lane = lax.broadcasted_iota(jnp.int32, (T, 128), 1)
        zero = jnp.zeros_like(q)
        qn_ref[:T, :128] = jnp.where(lane < HEAD_DIM, q, zero)
        qn_ref[T:, :128] = jnp.where(lane >= HEAD_DIM, q, zero)
        key_blk = lane & (FAR_LANES - 1)
        qry_blk = i * nc + lax.broadcasted_iota(jnp.int32, (T, 128), 0) // SUB
        c = jnp.where(key_blk <= qry_blk - 2, cfar_ref[h, 0], jnp.where(key_blk >= qry_blk + 2, cfar_ref[h, 1], 0.0))
        c_hi = c.astype(bf16)
        c_lo = (c - c_hi.astype(f32)).astype(bf16)
        ext = jnp.where(lane < FAR_LANES, c_hi, c_lo)
        qn_ref[:T, 128:] = ext
        qn_ref[T:, 128:] = ext

    def reset(state):
        acc_ref, m_ref = state
        acc_ref[...] = jnp.zeros_like(acc_ref)
        m_ref[...] = jnp.full_like(m_ref, NEG)

    def finish(i, state):
        acc = state[0][...]
        o1 = acc[:T, :A_VDIM]
        l1 = acc[:T, A_VDIM:A_VDIM + 1]
        o2 = acc[T:, :A_VDIM]
        l2 = acc[T:, A_VDIM:A_VDIM + 1]
        o = o1 / l1 - lam * (o2 / l2)
        o_ref[0, pl.ds(pl.multiple_of(i * T, T), T), :] = (_rms(o, sg_ref[...]) * (1.0 - lam_init)).astype(bf16)

    def scores(j, queries_ref):
        k0 = pl.multiple_of(j * T, T)
        return lax.dot_general(queries_ref[...], kext_ref[pl.ds(k0, T), :],
                               (((1,), (1,)), ((), ())), preferred_element_type=f32)

    def near_tiles(i, slot):
        if slot == 0:
            return {(0, nc - 1): jnp.where(i >= 1, 0, 3)}
        if slot == 2:
            return {(nc - 1, 0): jnp.where(i <= nk - 2, 2, 3)}
        return {(rb, cb): cb - rb + 1 for rb in range(nc) for cb in range(nc) if abs(cb - rb) <= 1}

    def softmax_pv(i, j, s_ref, slot, state):
        acc_ref, m_ref = state
        k0 = pl.multiple_of(j * T, T)
        tiles = {} if slot is None else near_tiles(i, slot)
        cols = []
        for cb in range(nc):
            lanes = slice(cb * 128, (cb + 1) * 128)
            if any(key[1] == cb for key in tiles):
                pieces = []
                for half in range(2):
                    for rb in range(nc):
                        r0 = half * T + rb * SUB
                        piece = s_ref[r0:r0 + SUB, lanes]
                        if (rb, cb) in tiles:
                            piece = piece + bt_ref[0, tiles[(rb, cb)]]
                        pieces.append(piece)
                cols.append(jnp.concatenate(pieces, axis=0))
            else:
                cols.append(s_ref[:, lanes])
        mx = cols[0]
        for n in range(1, nc):
            mx = jnp.maximum(mx, cols[n])
        m_prev = m_ref[...]
        m_new = jnp.maximum(m_prev, jnp.broadcast_to(jnp.max(mx, axis=-1, keepdims=True), m_prev.shape))
        alpha = jnp.exp2(m_prev - m_new)
        p = jnp.concatenate([jnp.exp2(col - m_new).astype(bf16) for col in cols], axis=1)
        pv = jnp.dot(p, vext_ref[pl.ds(k0, T), :], preferred_element_type=f32)
        acc_ref[:, :128] = alpha * acc_ref[:, :128] + pv[:, :128]
        acc_ref[:, 128:] = alpha * acc_ref[:, 128:] + pv[:, 128:]
        m_ref[...] = m_new

    def chunk_at(i, t):
        if isinstance(t, int) and t >= nk - 3:
            slot = t - (nk - 3)
            if slot == 0:
                return jnp.where(i >= 1, i - 1, nk - 1)
            if slot == 1:
                return i
            return jnp.where(i <= nk - 2, i + 1, nk - 3)
        lo = jnp.maximum(i - 1, 0)
        n_near = jnp.minimum(i + 2, nk) - lo
        return jnp.where(t < lo, t, t + n_near)

    bufs = (sa_ref, sb_ref)
    n_main = nk - PIPE_TAIL
    n_tiles = S // T

    def query_tile(i, q_cur_ref, q_nxt_ref, state, prev=None):
        def block(t0, n, tail):
            if tail and prev is not None:
                finish(*prev)
                reset(prev[1])
            for u in range(n):
                if not (tail and u == n - 1):
                    bufs[(u + 1) % 2][...] = scores(chunk_at(i, t0 + u + 1), q_cur_ref)
                elif q_nxt_ref is not None:
                    nxt = jnp.minimum(i + 1, n_tiles - 1)
                    stack_queries(nxt, q_nxt_ref)
                    bufs[(u + 1) % 2][...] = scores(chunk_at(nxt, 0), q_nxt_ref)
                softmax_pv(i, chunk_at(i, t0 + u), bufs[u % 2], u - (n - 3) if tail and u >= n - 3 else None, state)

        if n_main:
            unroll = max(u for u in range(2, PIPE_UNROLL + 1, 2) if n_main % u == 0)

            def main(g, c):
                block(g * unroll, unroll, False)
                return c

            lax.fori_loop(0, n_main // unroll, main, 0)
        block(n_main, PIPE_TAIL, True)
        if prev is None:
            finish(i, state)
            reset(state)

    even, odd = (acc_ref, m_ref), (acc2_ref, m2_ref)
    reset(even)
    reset(odd)
    main_trips = n_main // max(u for u in range(2, PIPE_UNROLL + 1, 2) if n_main % u == 0) if n_main else 0
    if main_trips >= 2:
        def tile_pair(g, carry):
            query_tile(2 * g, qs_ref, qn_ref, even, prev=(jnp.maximum(2 * g - 1, 0), odd))
            query_tile(2 * g + 1, qn_ref, qs_ref, odd, prev=(2 * g, even))
            return carry

        acc2_ref[:, A_VDIM:] = jnp.ones((2 * T, 128), f32)
        stack_queries(0, qs_ref)
        sa_ref[...] = scores(chunk_at(0, 0), qs_ref)
        lax.fori_loop(0, n_tiles // 2, tile_pair, 0)
        finish(n_tiles - 1, odd)
    else:
        def single_tile(i, carry):
            stack_queries(i, qs_ref)
            sa_ref[...] = scores(chunk_at(i, 0), qs_ref)
            query_tile(i, qs_ref, None, even)
            return carry

        lax.fori_loop(0, n_tiles, single_tile, 0)


def _attn_a(qa, ka, va, btiles, cfar, lamv, sg, lam_init):
    B, H, S, _ = qa.shape
    T = ATT_T
    assert S % (2 * T) == 0 and PIPE_TAIL <= S // T and S // SUB <= FAR_LANES and T % SUB == 0
    kernel = functools.partial(_attn_a_kernel, T=T, S=S, lam_init=lam_init)
    return pl.pallas_call(
        kernel,
        grid=(B, H),
        in_specs=[
            pl.BlockSpec(memory_space=pltpu.SMEM),
            pl.BlockSpec((1, 1, S, 128), lambda b, h: (b, h, 0, 0)),
            pl.BlockSpec((1, 1, S, 128), lambda b, h: (b, h, 0, 0)),
            pl.BlockSpec((1, 1, S, 128), lambda b, h: (b, h, 0, 0)),
            pl.BlockSpec((1, 4, SUB, SUB), lambda b, h: (h, 0, 0, 0)),
            pl.BlockSpec((4, HEAD_DIM), lambda b, h: (0, 0)),
            pl.BlockSpec((1, A_VDIM), lambda b, h: (0, 0)),
        ],
        out_specs=pl.BlockSpec((1, S, A_VDIM), lambda b, h: (b, 0, h)),
        out_shape=jax.ShapeDtypeStruct((B, S, A_V), bf16),
        scratch_shapes=[
            pltpu.VMEM((S, 256), bf16),
            pltpu.VMEM((S, 2 * A_VDIM), bf16),
            pltpu.VMEM((2 * T, 256), bf16),
            pltpu.VMEM((2 * T, 256), bf16),
            pltpu.VMEM((2 * T, T), f32),
            pltpu.VMEM((2 * T, T), f32),
            pltpu.VMEM((2 * T, 2 * A_VDIM), f32),
            pltpu.VMEM((2 * T, 128), f32),
            pltpu.VMEM((2 * T, 2 * A_VDIM), f32),
            pltpu.VMEM((2 * T, 128), f32),
        ],
        compiler_params=pltpu.CompilerParams(
            dimension_semantics=("parallel", "parallel"), vmem_limit_bytes=VMEM_LIMIT),
        name="attn_a",
    )(cfar, qa, ka, va, btiles, lamv, sg)


def _head_rows_attention(q, k, v, bias, nh):
    n = q.shape[0]
    rows = nh * n
    row_head = lax.broadcasted_iota(jnp.int32, (rows, LANE_BLOCK), 0) // n
    lane_head = lax.broadcasted_iota(jnp.int32, (rows, LANE_BLOCK), 1) // HEAD_DIM
    own = row_head == lane_head
    qs = jnp.where(own, jnp.concatenate([q] * nh, axis=0), jnp.zeros((rows, LANE_BLOCK), q.dtype))
    s = lax.dot_general(qs, k, (((1,), (1,)), ((), ())), preferred_element_type=f32)
    if bias is not None:
        s = s + bias
    m = jnp.max(s, axis=-1, keepdims=True)
    e = jnp.exp(s - m)
    l = jnp.sum(e, axis=-1, keepdims=True)
    r = jnp.dot(e.astype(bf16), v, preferred_element_type=f32) / l
    r = jnp.where(own, r, 0.0)
    out = r[:n]
    for hh in range(1, nh):
        out = out + r[hh * n:(hh + 1) * n]
    return out


def _local_kernel(qb_ref, kb_ref, vb_ref, bt_ref, qc_ref, kc_ref, vc_ref, o_ref, *, R):
    g = pl.program_id(1)
    kh = NA_ROWS

    def row(rr, c):
        r = g * LOCAL_ROWS + rr
        rs = jnp.clip(r - kh // 2, 0, R - kh)
        q0 = pl.multiple_of(rr * GRID_W, GRID_W)
        k0 = pl.multiple_of(rs * GRID_W, GRID_W)
        q = qb_ref[0, pl.ds(q0, GRID_W), :]
        k = kb_ref[0, pl.ds(k0, kh * GRID_W), :]
        v = vb_ref[0, pl.ds(k0, kh * GRID_W), :]
        o = _head_rows_attention(q, k, v, bt_ref[r - rs], B_HEADS)
        o_ref[0, pl.ds(q0, GRID_W), :B_W] = o.astype(bf16)
        return c

    lax.fori_loop(0, LOCAL_ROWS, row, 0, unroll=LOCAL_UNROLL)

    kc = kc_ref[0]
    vc = vc_ref[0]

    def mem_tile(t, c):
        q0 = pl.multiple_of(t * MEM_Q_TILE, MEM_Q_TILE)
        o = _head_rows_attention(qc_ref[0, pl.ds(q0, MEM_Q_TILE), :], kc, vc, None, C_HEADS)
        o_ref[0, pl.ds(q0, MEM_Q_TILE), B_W:] = o.astype(bf16)
        return c

    lax.fori_loop(0, LOCAL_ROWS * GRID_W // MEM_Q_TILE, mem_tile, 0, unroll=True)


def _local(qb, kb, vb, na_tiles, qc, kc, vc):
    B, S, _ = qb.shape
    R = S // GRID_W
    rows = LOCAL_ROWS * GRID_W
    tile = pl.BlockSpec((1, rows, LANE_BLOCK), lambda b, g: (b, g, 0))
    full = pl.BlockSpec((1, S, LANE_BLOCK), lambda b, g: (b, 0, 0))
    memspec = pl.BlockSpec((1, MEM_TOKENS, LANE_BLOCK), lambda b, g: (b, 0, 0))
    return pl.pallas_call(
        functools.partial(_local_kernel, R=R),
        grid=(B, R // LOCAL_ROWS),
        in_specs=[tile, full, full,
                  pl.BlockSpec(na_tiles.shape, lambda b, g: (0, 0, 0)),
                  tile, memspec, memspec],
        out_specs=pl.BlockSpec((1, rows, B_W + C_W), lambda b, g: (b, g, 0)),
        out_shape=jax.ShapeDtypeStruct((B, S, B_W + C_W), bf16),
        compiler_params=pltpu.CompilerParams(
            dimension_semantics=("parallel", "parallel"), vmem_limit_bytes=VMEM_LIMIT),
        name="local",
    )(qb, kb, vb, na_tiles, qc, kc, vc)


def _ffn_kernel(x_ref, xp_ref, xn_ref, a_ref, ap_ref, an_ref, l_ref, lp_ref, ln_ref, wa_ref, wl_ref, g2_ref,
                wv_ref, wg_ref, wd_ref, cp_ref, o_ref, h_ref, act_ref, uv_ref, ug_ref, *, T, nT, nC):
    i = pl.program_id(1)
    g2 = g2_ref[...]
    x_ext = jnp.concatenate([xp_ref[0], x_ref[0], xn_ref[0]], axis=0)
    a_ext = jnp.concatenate([ap_ref[0], a_ref[0], an_ref[0]], axis=0)
    l_ext = jnp.concatenate([lp_ref[0], l_ref[0], ln_ref[0]], axis=0)
    x1_ext = (x_ext + jnp.dot(a_ext, wa_ref[...], preferred_element_type=f32)
              + jnp.dot(l_ext, wl_ref[...], preferred_element_type=f32))
    x = x1_ext[MIX_HALO:MIX_HALO + T]
    hp = jnp.where(i == 0, 0.0, _rms(x1_ext[MIX_HALO - HALO:MIX_HALO], g2))
    hn = jnp.where(i == nT - 1, 0.0, _rms(x1_ext[MIX_HALO + T:MIX_HALO + T + HALO], g2))
    h_ref[...] = jnp.concatenate([hp, _rms(x, g2), hn], axis=0).astype(bf16)

    def conv(u_ref, cp, r0, rb):
        return (cp[rb:rb + 1] + cp[r0:r0 + 1] * u_ref[HALO - 1:HALO - 1 + T]
                + cp[r0 + 1:r0 + 2] * u_ref[HALO:HALO + T] + cp[r0 + 2:r0 + 3] * u_ref[HALO + 1:HALO + 1 + T])

    def up(c, slot):
        h = h_ref[...]
        uv_ref[slot] = jnp.dot(h, wv_ref[c], preferred_element_type=f32)
        ug_ref[slot] = jnp.dot(h, wg_ref[c], preferred_element_type=f32)

    def activate(c, slot):
        cp = cp_ref[c]
        val = conv(uv_ref.at[slot], cp, 0, 6)
        gate = conv(ug_ref.at[slot], cp, 3, 7)
        act_ref[:, c * FF_CHUNK:(c + 1) * FF_CHUNK] = (gate * jax.nn.sigmoid(gate) * val).astype(bf16)

    up(0, 0)
    for c in range(nC):
        if c + 1 < nC:
            up(c + 1, (c + 1) % 2)
        activate(c, c % 2)
    o_ref[0] = x + jnp.dot(act_ref[...], wd_ref[...], preferred_element_type=f32)


def _ffn(x, oa, ol, wa, wl, g2, wv, wg, wd, cp):
    B, S, _ = x.shape
    T = FFN_ROWS
    nT = S // T
    nC = wv.shape[0]
    hb = T // MIX_HALO
    const2 = lambda b, i: (0, 0)
    const3 = lambda b, i: (0, 0, 0)
    single = dict(pipeline_mode=pl.Buffered(1))
    tile = lambda w: pl.BlockSpec((1, T, w), lambda b, i: (b, i, 0))
    prev = lambda w: pl.BlockSpec((1, MIX_HALO, w), lambda b, i: (b, jnp.maximum(i * hb - 1, 0), 0))
    nxt = lambda w: pl.BlockSpec((1, MIX_HALO, w),
                                 lambda b, i: (b, jnp.minimum((i + 1) * hb, S // MIX_HALO - 1), 0))
    wa_w, wl_w = oa.shape[-1], ol.shape[-1]
    return pl.pallas_call(
        functools.partial(_ffn_kernel, T=T, nT=nT, nC=nC),
        grid=(B, nT),
        in_specs=[
            tile(D_MODEL), prev(D_MODEL), nxt(D_MODEL),
            tile(wa_w), prev(wa_w), nxt(wa_w),
            tile(wl_w), prev(wl_w), nxt(wl_w),
            pl.BlockSpec(wa.shape, const2, **single),
            pl.BlockSpec(wl.shape, const2, **single),
            pl.BlockSpec((1, D_MODEL), const2),
            pl.BlockSpec(wv.shape, const3, **single),
            pl.BlockSpec(wg.shape, const3, **single),
            pl.BlockSpec(wd.shape, const2, **single),
            pl.BlockSpec(cp.shape, const3),
        ],
        out_specs=pl.BlockSpec((1, T, D_MODEL), lambda b, i: (b, i, 0)),
        out_shape=jax.ShapeDtypeStruct((B, S, D_MODEL), f32),
        scratch_shapes=[pltpu.VMEM((T + 2 * HALO, D_MODEL), bf16), pltpu.VMEM((T, D_FF), bf16),
                        pltpu.VMEM((2, T + 2 * HALO, FF_CHUNK), f32),
                        pltpu.VMEM((2, T + 2 * HALO, FF_CHUNK), f32)],
        compiler_params=pltpu.CompilerParams(
            dimension_semantics=("parallel", "parallel"), vmem_limit_bytes=VMEM_LIMIT),
        name="ffn",
    )(x, x, x, oa, oa, oa, ol, ol, ol, wa, wl, g2, wv, wg, wd, cp)


def _t5_bucket(rp):
    half = T5_BUCKETS // 2
    max_exact = half // 2
    ret = jnp.where(rp > 0, half, 0)
    n = jnp.abs(rp)
    nf = jnp.maximum(n, 1).astype(f32)
    large = max_exact + (jnp.log(nf / max_exact) / math.log(T5_MAX_DIST / max_exact)
                         * (half - max_exact)).astype(jnp.int32)
    large = jnp.minimum(large, half - 1)
    return ret + jnp.where(n < max_exact, n, large)


def _t5_tiles(rel_bias, T):
    assert T >= T5_MAX_DIST
    off = np.arange(T)[None, :] - np.arange(T)[:, None]
    rp = np.stack([off - T, off, off + T]).astype(np.int32)
    bucket = _t5_bucket(jnp.asarray(rp))[None]
    table = jnp.transpose(rel_bias).astype(f32) * LOG2E
    tiles = jnp.zeros((table.shape[0], 3, T, T), f32)
    for b in range(T5_BUCKETS):
        tiles = jnp.where(bucket == b, table[:, b][:, None, None, None], tiles)
    tiles = jnp.concatenate([tiles, jnp.zeros_like(tiles[:, :1])], axis=1)
    far_bucket = _t5_bucket(jnp.asarray([-2 * T, 2 * T], jnp.int32))
    far = jnp.zeros((table.shape[0], 2), f32)
    for b in range(T5_BUCKETS):
        far = jnp.where(far_bucket[None, :] == b, table[:, b][:, None], far)
    return tiles, far


def _na_tiles(na_bias_l):
    kh = NA_ROWS
    c = np.arange(GRID_W)
    cs = np.clip(c - NA_COLS // 2, 0, GRID_W - NA_COLS)
    kc = np.arange(GRID_W)
    valid = (kc[None, :] >= cs[:, None]) & (kc[None, :] < cs[:, None] + NA_COLS)
    dcol = kc[None, :] - c[:, None] + (NA_COLS - 1)
    delta = np.arange(kh)
    ki = np.arange(kh)
    drow = ki[None, :] - delta[:, None] + (NA_ROWS - 1)
    row_sel = (drow[:, :, None] == np.arange(2 * NA_ROWS - 1)).astype(np.float32)
    col_sel = (dcol[None] == np.arange(2 * NA_COLS - 1)[:, None, None]).astype(np.float32)
    hp = lax.Precision.HIGHEST
    tmp = jnp.einsum('hab,bck->hack', na_bias_l.astype(f32), col_sel, precision=hp)
    vals = jnp.einsum('dia,hack->dhcik', row_sel, tmp, precision=hp)
    vals = jnp.where(valid[None, None, :, None, :], vals, NEG)
    return vals.reshape(kh, B_HEADS * GRID_W, kh * GRID_W)


def _in_perm():
    q1, q2, k1, k2 = 0, A_QK, 2 * A_QK, 3 * A_QK
    cols = []
    for a, b in ((q1, q2), (k1, k2)):
        for h in range(A_HEADS):
            cols += list(range(a + h * HEAD_DIM, a + (h + 1) * HEAD_DIM))
            cols += list(range(b + h * HEAD_DIM, b + (h + 1) * HEAD_DIM))
    cols += list(range(4 * A_QK, IN_WIDTH))
    return np.asarray(cols, np.int32)


def _layer_params(l, p):
    scale = HEAD_DIM ** -0.5
    ones = lambda n: jnp.ones((n,), f32)
    tile = lambda g, n: jnp.tile(g.astype(f32), n)
    gain = jnp.concatenate([
        tile(p['qn_a'][l], 2 * A_HEADS) * (scale * LOG2E), tile(p['kn_a'][l], 2 * A_HEADS), ones(A_V),
        tile(p['qn_b'][l], B_HEADS) * scale, tile(p['kn_b'][l], B_HEADS), ones(B_W),
        tile(p['qn_c'][l], C_HEADS) * scale])[None, :]
    nC = D_FF // FF_CHUNK
    w_up = p['w_up'][l].astype(bf16)
    chunks = lambda w: jnp.transpose(w.reshape(D_MODEL, nC, FF_CHUNK), (1, 0, 2))
    cw = p['conv_w'][l].astype(f32)
    cb = p['conv_b'][l].astype(f32)
    cp = jnp.concatenate([cw[:, :D_FF], cw[:, D_FF:], cb[None, :D_FF], cb[None, D_FF:]], axis=0)
    w_out = p['w_out'][l].astype(bf16)
    return dict(
        g1=p['norm1_g'][l].astype(f32)[None, :],
        w_in=p['w_in'][l][:, _in_perm()].astype(bf16),
        gain=gain,
        lamv=jnp.stack([p['lam_q1'][l], p['lam_k1'][l], p['lam_q2'][l], p['lam_k2'][l]]).astype(f32),
        sg=p['subln_g'][l].astype(f32)[None, :],
        na=_na_tiles(p['na_bias'][l]),
        mem_g=p['mem_g'][l].astype(f32)[None, :],
        w_mem=p['w_mem_kv'][l].astype(bf16),
        gain_c=tile(p['kn_c'][l], C_HEADS)[None, :],
        wo=w_out,
        g2=p['norm2_g'][l].astype(f32)[None, :],
        wv=chunks(w_up[:, :D_FF]), wg=chunks(w_up[:, D_FF:]),
        wd=p['w_down'][l].astype(bf16),
        cp=jnp.transpose(cp.reshape(8, nC, FF_CHUNK), (1, 0, 2)),
        lam_init=0.8 - 0.6 * math.exp(-0.3 * l),
    )


def _layer(x, mem, lp, shared):
    qa, ka, va, qb, kb, vb, qc = _proj(x, lp['g1'], lp['w_in'], shared['gmat'], lp['gain'])
    kc, vc = _mem_kv(mem, lp['mem_g'], lp['w_mem'], shared['gmat'], lp['gain_c'])
    oa = _attn_a(qa, ka, va, shared['btiles'], shared['cfar'], lp['lamv'], lp['sg'], lp['lam_init'])
    ol = _local(qb, kb, vb, lp['na'], qc, kc, vc)
    return _ffn(x, oa, ol, lp['wo'][:A_V], lp['wo'][A_V:], lp['g2'], lp['wv'], lp['wg'], lp['wd'], lp['cp'])


def kernel(x_prompt, x_sample, mem_prompt, mem_sample, norm1_g, w_in, qn_a, kn_a, lam_q1, lam_k1, lam_q2, lam_k2, subln_g, rel_bias, qn_b, kn_b, na_bias, mem_g, w_mem_kv, qn_c, kn_c, w_out, norm2_g, w_up, conv_w, conv_b, w_down):
    p = dict(norm1_g=norm1_g, w_in=w_in, qn_a=qn_a, kn_a=kn_a, lam_q1=lam_q1, lam_k1=lam_k1, lam_q2=lam_q2,
             lam_k2=lam_k2, subln_g=subln_g, qn_b=qn_b, kn_b=kn_b, na_bias=na_bias, mem_g=mem_g,
             w_mem_kv=w_mem_kv, qn_c=qn_c, kn_c=kn_c, w_out=w_out, norm2_g=norm2_g, w_up=w_up,
             conv_w=conv_w, conv_b=conv_b, w_down=w_down)
    depth = w_in.shape[0]
    group = np.arange(LANE_BLOCK) // HEAD_DIM
    gmat = jnp.asarray((group[:, None] == group[None, :]) / HEAD_DIM, bf16)
    btiles, cfar = _t5_tiles(rel_bias, SUB)
    shared = dict(gmat=gmat, btiles=btiles, cfar=cfar)
    layers = [_layer_params(l, p) for l in range(depth)]
    y_prompt, y_sample = x_prompt, x_sample
    for lp in layers:
        y_prompt = _layer(y_prompt, mem_prompt, lp, shared)
    for lp in layers:
        y_sample = _layer(y_sample, mem_sample, lp, shared)
    return (y_prompt, y_sample)
```
